```python
import functools
import numpy as np
import jax
import jax.numpy as jnp
from jax import lax

D_MODEL = 4096
BATCH = 4
SEQ = 2048
DEPTH = 2
DEC_BATCH = 8
DEC_SEQ = 4
PAST_LEN = 16384
PAGE_SIZE = 128

HEAD_DIM = 128
N_HEAD_TOTAL = D_MODEL // HEAD_DIM
ML_HEADS = N_HEAD_TOTAL // 4
HG_HEADS = N_HEAD_TOTAL // 4
FOX_HEADS = N_HEAD_TOTAL - ML_HEADS - HG_HEADS
ML_W = ML_HEADS * HEAD_DIM
HG_W = HG_HEADS * HEAD_DIM
FOX_W = FOX_HEADS * HEAD_DIM
CONV_W = 4
ML_CHUNK = 128
HG_CHUNK = 64
FOX_BLOCK = 128
N_MEM = 256
CA_HEADS = 4
CA_HEAD_DIM = 128
CA_W = CA_HEADS * CA_HEAD_DIM
D_FF = 7 * D_MODEL // 2
N_EXPERTS = 8
TOP_K = 2
EPS = 1e-6
SPLIT_SIZES = (2 * ML_W, ML_W, ML_W, ML_HEADS, ML_HEADS,
               HG_W, HG_W, HG_W, HG_W,
               FOX_W, FOX_W, FOX_W, FOX_HEADS)
SPLIT_IDX = tuple(int(s) for s in np.cumsum(SPLIT_SIZES)[:-1])
D_IN = sum(SPLIT_SIZES)

kernel_name = 'hybrid_mlstm_hgrn2_fox_step'


def rmsnorm(x, g):
    xf = x.astype(jnp.float32)
    y = xf * lax.rsqrt(jnp.mean(xf * xf, axis=-1, keepdims=True) + EPS)
    return (y * g.astype(jnp.float32)).astype(x.dtype)


def to_heads(a, n_heads):
    b, t, _ = a.shape
    return a.astype(jnp.float32).reshape(b, t, n_heads, HEAD_DIM).transpose(0, 2, 1, 3)


def causal_conv(u, buf, w, bias):
    t = u.shape[1]
    full = jnp.concatenate([buf, u], axis=1)
    out = bias + sum(full[:, j:j + t] * w[j] for j in range(CONV_W))
    return out, full[:, full.shape[1] - (CONV_W - 1):]


def chunked_scan(step, carry, xs, chunk):
    t = xs[0].shape[2]
    size = chunk if t % chunk == 0 else t
    nc = t // size

    def split(a):
        a = a.reshape(a.shape[:2] + (nc, size) + a.shape[3:])
        return jnp.moveaxis(a, 2, 0)

    carry, ys = lax.scan(step, carry, tuple(split(a) for a in xs))
    ys = jnp.moveaxis(ys, 0, 2)
    return carry, ys.reshape(ys.shape[:2] + (t,) + ys.shape[4:])


def mlstm_chunk(carry, xs):
    c_prev, n_prev, m_prev = carry
    q, k, v, li, lf = xs
    size = q.shape[2]
    causal = jnp.tril(jnp.ones((size, size), dtype=bool))
    b = jnp.cumsum(lf, axis=-1)
    logw = jnp.where(causal, b[..., :, None] - b[..., None, :] + li[..., None, :], -jnp.inf)
    g = b + m_prev[..., None]
    m_t = jnp.maximum(jnp.max(logw, axis=-1), g)
    scores = jnp.einsum('bhtd,bhsd->bhts', q, k) * jnp.exp(logw - m_t[..., None])
    inter = jnp.exp(g - m_t)
    num = jnp.einsum('bhts,bhsd->bhtd', scores, v) + inter[..., None] * jnp.einsum('bhtd,bhde->bhte', q, c_prev)
    den = jnp.sum(scores, axis=-1) + inter * jnp.einsum('bhtd,bhd->bht', q, n_prev)
    h = num / jnp.maximum(jnp.abs(den), jnp.exp(-m_t))[..., None]
    m_new = m_t[..., -1]
    wk = jnp.exp(b[..., -1:] - b + li - m_new[..., None])
    decay = jnp.exp(b[..., -1] + m_prev - m_new)
    c_new = decay[..., None, None] * c_prev + jnp.einsum('bhs,bhsd,bhse->bhde', wk, k, v)
    n_new = decay[..., None] * n_prev + jnp.einsum('bhs,bhsd->bhd', wk, k)
    return (c_new, n_new, m_new), h


def hgrn2_chunk(s_prev, xs):
    q, k, v, lf = xs
    size = q.shape[2]
    causal = jnp.tril(jnp.ones((size, size), dtype=bool))
    b = jnp.cumsum(lf, axis=2)
    rel = b[:, :, :, None, :] - b[:, :, None, :, :]
    dec = jnp.exp(jnp.where(causal[:, :, None], rel, -jnp.inf))
    a = jnp.einsum('bhtd,bhsd,bhtsd->bhts', q, k, dec)
    o = jnp.einsum('bhts,bhse->bhte', a, v) + jnp.einsum('bhtd,bhde->bhte', q * jnp.exp(b), s_prev)
    b_last = b[:, :, -1:, :]
    s_new = jnp.exp(b_last[:, :, 0, :])[..., None] * s_prev + jnp.einsum('bhsd,bhse->bhde', k * jnp.exp(b_last - b), v)
    return s_new, o


def fox_prefill(q, k, v, lf):
    _, t, _, d = q.shape
    blk = FOX_BLOCK if t % FOX_BLOCK == 0 else t
    f_cum = jnp.cumsum(lf, axis=1).transpose(0, 2, 1)
    qh = q.transpose(0, 2, 1, 3) * d ** -0.5
    kh = k.transpose(0, 2, 1, 3)
    vh = v.transpose(0, 2, 1, 3)
    outs = []
    for i in range(t // blk):
        q0, q1 = i * blk, (i + 1) * blk
        s = jnp.einsum('bhtd,bhsd->bhts', qh[:, :, q0:q1], kh[:, :, :q1])
        s = s + f_cum[:, :, q0:q1, None] - f_cum[:, :, None, :q1]
        mask = jnp.arange(q1)[None, :] <= jnp.arange(q0, q1)[:, None]
        p = jax.nn.softmax(jnp.where(mask, s, -jnp.inf), axis=-1)
        outs.append(jnp.einsum('bhts,bhsd->bhtd', p, vh[:, :, :q1]))
    return jnp.concatenate(outs, axis=2).transpose(0, 2, 1, 3)


def fox_decode(q, k, v, lf, k_pool, v_pool, lf_pool, page_table):
    bd, t, nh, d = q.shape
    n_pages = page_table.shape[1]
    page = k_pool.shape[1]
    f32 = jnp.float32
    lf_past = lf_pool[page_table].astype(f32).reshape(bd, n_pages * page, nh)
    suffix = lax.cumsum(lf_past, axis=1, reverse=True) - lf_past
    suffix_pages = jnp.moveaxis(suffix.reshape(bd, n_pages, page, nh), 1, 0)
    g_new = jnp.cumsum(lf, axis=1).transpose(0, 2, 1)
    qh = q.transpose(0, 2, 1, 3) * d ** -0.5

    def step(carry, xs):
        m, l, acc = carry
        phys, suf = xs
        kb = k_pool[phys].astype(f32)
        vb = v_pool[phys].astype(f32)
        s = jnp.einsum('bhtd,bshd->bhts', qh, kb) + g_new[..., None] + suf.transpose(0, 2, 1)[:, :, None, :]
        m_new = jnp.maximum(m, jnp.max(s, axis=-1))
        corr = jnp.exp(m - m_new)
        p = jnp.exp(s - m_new[..., None])
        return (m_new, l * corr + jnp.sum(p, axis=-1),
                acc * corr[..., None] + jnp.einsum('bhts,bshd->bhtd', p, vb)), None

    init = (jnp.full((bd, nh, t), -jnp.inf, f32), jnp.zeros((bd, nh, t), f32), jnp.zeros((bd, nh, t, d), f32))
    (m, l, acc), _ = lax.scan(step, init, (page_table.T, suffix_pages))
    s = jnp.einsum('bhtd,bshd->bhts', qh, k) + g_new[..., :, None] - g_new[..., None, :]
    s = jnp.where(jnp.tril(jnp.ones((t, t), dtype=bool)), s, -jnp.inf)
    m_new = jnp.maximum(m, jnp.max(s, axis=-1))
    corr = jnp.exp(m - m_new)
    p = jnp.exp(s - m_new[..., None])
    l = l * corr + jnp.sum(p, axis=-1)
    acc = acc * corr[..., None] + jnp.einsum('bhts,bshd->bhtd', p, v)
    return (acc / l[..., None]).transpose(0, 2, 1, 3)


def mixer(h, lb, w_in, b_in, conv_w, conv_b, ml_norm_w, hg_norm_w, w_out, conv_buf, ml_state, hg_state, fox_fn):
    bsz, t, _ = h.shape
    f32 = jnp.float32
    z = h @ w_in + b_in
    (ml_qk, ml_v, ml_o, ml_i, ml_f, hg_q, hg_f, hg_i, hg_g, fx_q, fx_k, fx_v, fx_f) = jnp.split(z, SPLIT_IDX, axis=-1)
    qk_conv, conv_new = causal_conv(ml_qk, conv_buf.astype(z.dtype), conv_w, conv_b)
    qk_conv = jax.nn.silu(qk_conv.astype(f32))
    q = to_heads(qk_conv[..., :ML_W], ML_HEADS)
    k = to_heads(qk_conv[..., ML_W:], ML_HEADS) * HEAD_DIM ** -0.5
    v = to_heads(ml_v, ML_HEADS)
    li = ml_i.astype(f32).transpose(0, 2, 1)
    lf = jax.nn.log_sigmoid(ml_f.astype(f32)).transpose(0, 2, 1)
    ml_carry = tuple(s.astype(f32) for s in ml_state)
    ml_new, h_ml = chunked_scan(mlstm_chunk, ml_carry, (q, k, v, li, lf), ML_CHUNK)
    h_ml = rmsnorm(h_ml.transpose(0, 2, 1, 3), ml_norm_w).reshape(bsz, t, ML_W) * jax.nn.sigmoid(ml_o.astype(f32))
    fg = lb + (1.0 - lb) * jax.nn.sigmoid(hg_f.astype(f32))
    hq = to_heads(jax.nn.silu(hg_q.astype(f32)), HG_HEADS)
    hk = to_heads(1.0 - fg, HG_HEADS)
    hv = to_heads(hg_i, HG_HEADS)
    hlf = to_heads(jnp.log(fg), HG_HEADS)
    hg_new, o_hg = chunked_scan(hgrn2_chunk, hg_state.astype(f32), (hq, hk, hv, hlf), HG_CHUNK)
    o_hg = rmsnorm(o_hg.transpose(0, 2, 1, 3), hg_norm_w).reshape(bsz, t, HG_W) * jax.nn.silu(hg_g.astype(f32))
    fq = fx_q.astype(f32).reshape(bsz, t, FOX_HEADS, HEAD_DIM)
    fk = fx_k.astype(f32).reshape(bsz, t, FOX_HEADS, HEAD_DIM)
    fv = fx_v.astype(f32).reshape(bsz, t, FOX_HEADS, HEAD_DIM)
    flf = jax.nn.log_sigmoid(fx_f.astype(f32))
    o_fx = fox_fn(fq, fk, fv, flf).reshape(bsz, t, FOX_W)
    merged = jnp.concatenate([h_ml, o_hg, o_fx], axis=-1).astype(h.dtype)
    return merged @ w_out, conv_new, ml_new, hg_new, (fk, fv, flf)


def mem_kv(mem, g, w_k, w_v):
    b, m, _ = mem.shape
    hm = rmsnorm(mem, g)
    return ((hm @ w_k).reshape(b, m, CA_HEADS, CA_HEAD_DIM), (hm @ w_v).reshape(b, m, CA_HEADS, CA_HEAD_DIM))


def cross_attn(h, mem_k, mem_v, w_q, w_o):
    b, t, _ = h.shape
    q = (h @ w_q).astype(jnp.float32).reshape(b, t, CA_HEADS, CA_HEAD_DIM) * CA_HEAD_DIM ** -0.5
    p = jax.nn.softmax(jnp.einsum('bthd,bmhd->bhtm', q, mem_k.astype(jnp.float32)), axis=-1)
    o = jnp.einsum('bhtm,bmhd->bthd', p, mem_v.astype(jnp.float32)).reshape(b, t, CA_W)
    return o.astype(h.dtype) @ w_o


def swiglu(h, w1, w3, w2):
    return (jax.nn.silu(h @ w1) * (h @ w3)) @ w2


def moe(h, w_router, b_router, w1, w3, w2):
    logits = (h @ w_router).astype(jnp.float32) + b_router.astype(jnp.float32)
    top_val, top_idx = lax.top_k(logits, TOP_K)
    gates = jax.nn.softmax(top_val, axis=-1)
    combine = jnp.sum(jax.nn.one_hot(top_idx, N_EXPERTS, dtype=jnp.float32) * gates[..., None], axis=-2)
    out = jnp.zeros_like(h)
    for e in range(N_EXPERTS):
        out = out + combine[..., e:e + 1].astype(h.dtype) * swiglu(h, w1[e], w3[e], w2[e])
    return out


def trunk_layer(x, mem_k, mem_v, conv_buf, ml_state, hg_state, fox_fn, ffn, lb, lw):
    mix, conv_new, ml_new, hg_new, fox_rows = mixer(
        rmsnorm(x, lw['n_mix_pre']), lb, lw['w_in'], lw['b_in'], lw['conv_w'], lw['conv_b'],
        lw['ml_norm_w'], lw['hg_norm_w'], lw['w_out'], conv_buf, ml_state, hg_state, fox_fn)
    x = x + rmsnorm(mix, lw['n_mix_post'])
    c = cross_attn(rmsnorm(x, lw['n_ca_pre']), mem_k, mem_v, lw['ca_w_q'], lw['ca_w_o'])
    x = x + rmsnorm(c, lw['n_ca_post'])
    x = x + rmsnorm(ffn(rmsnorm(x, lw['n_ffn_pre'])), lw['n_ffn_post'])
    return x, conv_new, ml_new, hg_new, fox_rows


def setup_inputs(seed: int = 0) -> dict:
    key = jax.random.key(seed)
    keys = iter(jax.random.split(key, 48))

    def nrm(shape, scale):
        return jax.random.normal(next(keys), shape, jnp.float32) * scale

    n_pages = PAST_LEN // PAGE_SIZE
    n_pool = (DEC_BATCH * n_pages * 5) // 4
    n_dense = (DEPTH + 1) // 2
    n_moe = DEPTH // 2
    starts = (0,) + SPLIT_IDX
    offset = np.zeros((D_IN,), np.float32)
    offset[starts[4]:starts[4] + ML_HEADS] = np.linspace(3.0, 6.0, ML_HEADS)
    offset[starts[12]:starts[12] + FOX_HEADS] = np.linspace(2.0, 5.0, FOX_HEADS)
    page_table = jax.random.permutation(next(keys), n_pool)[:DEC_BATCH * n_pages]
    page_table = page_table.reshape(DEC_BATCH, n_pages).astype(jnp.int32)
    sd = D_MODEL ** -0.5
    return {
        'x_prompt': nrm((BATCH, SEQ, D_MODEL), 1.0),
        'x_sample': nrm((DEC_BATCH, DEC_SEQ, D_MODEL), 1.0),
        'cache_fox_k': nrm((DEPTH, n_pool, PAGE_SIZE, FOX_HEADS, HEAD_DIM), 1.0),
        'cache_fox_v': nrm((DEPTH, n_pool, PAGE_SIZE, FOX_HEADS, HEAD_DIM), 1.0),
        'cache_fox_logf': jax.nn.log_sigmoid(nrm((DEPTH, n_pool, PAGE_SIZE, FOX_HEADS), 1.0) + 3.0),
        'cache_mem_k': nrm((DEPTH, DEC_BATCH, N_MEM, CA_HEADS, CA_HEAD_DIM), 1.0),
        'cache_mem_v': nrm((DEPTH, DEC_BATCH, N_MEM, CA_HEADS, CA_HEAD_DIM), 1.0),
        'state_ml_conv': nrm((DEPTH, DEC_BATCH, CONV_W - 1, 2 * ML_W), 1.0),
        'state_ml_C': nrm((DEPTH, DEC_BATCH, ML_HEADS, HEAD_DIM, HEAD_DIM), 0.3),
        'state_ml_n': nrm((DEPTH, DEC_BATCH, ML_HEADS, HEAD_DIM), 0.3),
        'state_ml_m': nrm((DEPTH, DEC_BATCH, ML_HEADS), 1.0),
        'state_hg_S': nrm((DEPTH, DEC_BATCH, HG_HEADS, HEAD_DIM, HEAD_DIM), 0.3),
        'page_table': page_table,
        'mem_prompt': nrm((BATCH, N_MEM, D_MODEL), 1.0),
        'w_in': nrm((DEPTH, D_MODEL, D_IN), sd),
        'b_in': nrm((DEPTH, D_IN), 0.02) + jnp.asarray(offset),
        'ml_conv_w': nrm((DEPTH, CONV_W, 2 * ML_W), 0.5),
        'ml_conv_b': nrm((DEPTH, 2 * ML_W), 0.02),
        'ml_norm_w': 1.0 + nrm((DEPTH, HEAD_DIM), 0.02),
        'hg_lower_bounds': nrm((DEPTH, HG_W), 0.1),
        'hg_norm_w': 1.0 + nrm((DEPTH, HEAD_DIM), 0.02),
        'w_out': nrm((DEPTH, D_MODEL, D_MODEL), sd),
        'ca_mem_norm': 1.0 + nrm((DEPTH, D_MODEL), 0.02),
        'ca_w_q': nrm((DEPTH, D_MODEL, CA_W), sd),
        'ca_w_k': nrm((DEPTH, D_MODEL, CA_W), sd),
        'ca_w_v': nrm((DEPTH, D_MODEL, CA_W), sd),
        'ca_w_o': nrm((DEPTH, CA_W, D_MODEL), CA_W ** -0.5),
        'norm_mix_pre': 1.0 + nrm((DEPTH, D_MODEL), 0.02),
        'norm_mix_post': 1.0 + nrm((DEPTH, D_MODEL), 0.02),
        'norm_ca_pre': 1.0 + nrm((DEPTH, D_MODEL), 0.02),
        'norm_ca_post': 1.0 + nrm((DEPTH, D_MODEL), 0.02),
        'norm_ffn_pre': 1.0 + nrm((DEPTH, D_MODEL), 0.02),
        'norm_ffn_post': 1.0 + nrm((DEPTH, D_MODEL), 0.02),
        'ffn_w1': nrm((n_dense, D_MODEL, D_FF), sd),
        'ffn_w3': nrm((n_dense, D_MODEL, D_FF), sd),
        'ffn_w2': nrm((n_dense, D_FF, D_MODEL), D_FF ** -0.5),
        'moe_router': nrm((n_moe, D_MODEL, N_EXPERTS), sd),
        'moe_router_b': nrm((n_moe, N_EXPERTS), 0.01),
        'moe_w1': nrm((n_moe, N_EXPERTS, D_MODEL, D_FF), sd),
        'moe_w3': nrm((n_moe, N_EXPERTS, D_MODEL, D_FF), sd),
        'moe_w2': nrm((n_moe, N_EXPERTS, D_FF, D_MODEL), D_FF ** -0.5),
    }


def reference(x_prompt, x_sample, cache_fox_k, cache_fox_v, cache_fox_logf, cache_mem_k, cache_mem_v,
              state_ml_conv, state_ml_C, state_ml_n, state_ml_m, state_hg_S, page_table, mem_prompt,
              w_in, b_in, ml_conv_w, ml_conv_b, ml_norm_w, hg_lower_bounds, hg_norm_w, w_out,
              ca_mem_norm, ca_w_q, ca_w_k, ca_w_v, ca_w_o,
              norm_mix_pre, norm_mix_post, norm_ca_pre, norm_ca_post, norm_ffn_pre, norm_ffn_post,
              ffn_w1, ffn_w3, ffn_w2, moe_router, moe_router_b, moe_w1, moe_w3, moe_w2):
    f32 = jnp.float32
    dt = x_prompt.dtype
    lb_soft = jax.nn.softmax(hg_lower_bounds.astype(f32), axis=0)
    lower_bounds = jnp.cumsum(lb_soft, axis=0) - lb_soft[0]
    bp = x_prompt.shape[0]
    yp, ys = x_prompt, x_sample
    names = ('p_k', 'p_v', 'p_lf', 'p_mk', 'p_mv', 'p_conv', 'p_C', 'p_n', 'p_m', 'p_S',
             's_k', 's_v', 's_lf', 's_conv', 's_C', 's_n', 's_m', 's_S')
    acc = {name: [] for name in names}
    for l in range(DEPTH):
        lw = {'w_in': w_in[l], 'b_in': b_in[l], 'conv_w': ml_conv_w[l], 'conv_b': ml_conv_b[l],
              'ml_norm_w': ml_norm_w[l], 'hg_norm_w': hg_norm_w[l], 'w_out': w_out[l],
              'ca_w_q': ca_w_q[l], 'ca_w_o': ca_w_o[l],
              'n_mix_pre': norm_mix_pre[l], 'n_mix_post': norm_mix_post[l],
              'n_ca_pre': norm_ca_pre[l], 'n_ca_post': norm_ca_post[l],
              'n_ffn_pre': norm_ffn_pre[l], 'n_ffn_post': norm_ffn_post[l]}
        j = l // 2
        if l % 2 == 0:
            ffn = functools.partial(swiglu, w1=ffn_w1[j], w3=ffn_w3[j], w2=ffn_w2[j])
        else:
            ffn = functools.partial(moe, w_router=moe_router[j], b_router=moe_router_b[j],
                                    w1=moe_w1[j], w3=moe_w3[j], w2=moe_w2[j])
        mk, mv = mem_kv(mem_prompt, ca_mem_norm[l], ca_w_k[l], ca_w_v[l])
        conv0 = jnp.zeros((bp, CONV_W - 1, 2 * ML_W), dt)
        ml0 = (jnp.zeros((bp, ML_HEADS, HEAD_DIM, HEAD_DIM), f32),
               jnp.zeros((bp, ML_HEADS, HEAD_DIM), f32),
               jnp.zeros((bp, ML_HEADS), f32))
        hg0 = jnp.zeros((bp, HG_HEADS, HEAD_DIM, HEAD_DIM), f32)
        yp, conv_p, ml_p, hg_p, fox_p = trunk_layer(yp, mk, mv, conv0, ml0, hg0, fox_prefill, ffn, lower_bounds[l], lw)
        for name, val in zip(('p_k', 'p_v', 'p_lf', 'p_mk', 'p_mv', 'p_conv', 'p_C', 'p_n', 'p_m', 'p_S'),
                             (fox_p[0], fox_p[1], fox_p[2], mk, mv, conv_p, ml_p[0], ml_p[1], ml_p[2], hg_p)):
            acc[name].append(val.astype(dt))
        fox_fn = functools.partial(fox_decode, k_pool=cache_fox_k[l], v_pool=cache_fox_v[l],
                                   lf_pool=cache_fox_logf[l], page_table=page_table)
        ys, conv_s, ml_s, hg_s, fox_s = trunk_layer(
            ys, cache_mem_k[l], cache_mem_v[l], state_ml_conv[l],
            (state_ml_C[l], state_ml_n[l], state_ml_m[l]), state_hg_S[l], fox_fn, ffn, lower_bounds[l], lw)
        for name, val in zip(('s_k', 's_v', 's_lf', 's_conv', 's_C', 's_n', 's_m', 's_S'),
                             (fox_s[0], fox_s[1], fox_s[2], conv_s, ml_s[0], ml_s[1], ml_s[2], hg_s)):
            acc[name].append(val.astype(dt))
    new_fox_k_p = jnp.stack(acc['p_k'])
    new_fox_v_p = jnp.stack(acc['p_v'])
    new_fox_logf_p = jnp.stack(acc['p_lf'])
    new_mem_k_p = jnp.stack(acc['p_mk'])
    new_mem_v_p = jnp.stack(acc['p_mv'])
    new_ml_conv_p = jnp.stack(acc['p_conv'])
    new_ml_C_p = jnp.stack(acc['p_C'])
    new_ml_n_p = jnp.stack(acc['p_n'])
    new_ml_m_p = jnp.stack(acc['p_m'])
    new_hg_S_p = jnp.stack(acc['p_S'])
    new_fox_k_s = jnp.stack(acc['s_k'])
    new_fox_v_s = jnp.stack(acc['s_v'])
    new_fox_logf_s = jnp.stack(acc['s_lf'])
    new_ml_conv_s = jnp.stack(acc['s_conv'])
    new_ml_C_s = jnp.stack(acc['s_C'])
    new_ml_n_s = jnp.stack(acc['s_n'])
    new_ml_m_s = jnp.stack(acc['s_m'])
    new_hg_S_s = jnp.stack(acc['s_S'])
    return (yp, ys,
            new_fox_k_p, new_fox_v_p, new_fox_logf_p, new_mem_k_p, new_mem_v_p,
            new_ml_conv_p, new_ml_C_p, new_ml_n_p, new_ml_m_p, new_hg_S_p,
            new_fox_k_s, new_fox_v_s, new_fox_logf_s,
            new_ml_conv_s, new_ml_C_s, new_ml_n_s, new_ml_m_s, new_hg_S_s)
```

```python
import functools

import jax
import jax.numpy as jnp
from jax import lax
from jax.experimental import pallas as pl
from jax.experimental.pallas import tpu as pltpu

F32 = jnp.float32
BF16 = jnp.bfloat16
EPS = 1e-6
HEAD_DIM = 128
CONV_W = 4
TOP_K = 2
SUB = 16
VMEM_LIMIT_BYTES = 56 * 1024 * 1024
HIGHEST = lax.Precision.HIGHEST
NEG_INF = float("-inf")


def _cparams(sem):
    return pltpu.CompilerParams(dimension_semantics=sem, vmem_limit_bytes=VMEM_LIMIT_BYTES)


def _round_up(a, b):
    return (a + b - 1) // b * b


def _pick_tile(n, target, quantum):
    if n <= quantum:
        return n
    return max(quantum, min(target, n) // quantum * quantum)


def _div_tile(n, target):
    if n <= target:
        return n
    t = target // 128 * 128
    while t > 128 and n % t:
        t -= 128
    assert n % t == 0, (n, target)
    return t


def _log_sigmoid(x):
    return jnp.minimum(x, 0.0) - jnp.log(1.0 + jnp.exp(-jnp.abs(x)))


def _sigmoid(x):
    return 1.0 / (1.0 + jnp.exp(-x))


def _silu(x):
    return x * _sigmoid(x)


def _dot_nt(a, b):
    return lax.dot_general(a, b, (((1,), (1,)), ((), ())), preferred_element_type=F32)


def _row_to_col(row, n):
    r = lax.broadcasted_iota(jnp.int32, (n, n), 0)
    c = lax.broadcasted_iota(jnp.int32, (n, n), 1)
    return jnp.sum(jnp.where(r == c, row, 0.0), axis=1, keepdims=True)


def _col_to_row(col, n):
    r = lax.broadcasted_iota(jnp.int32, (n, n), 0)
    c = lax.broadcasted_iota(jnp.int32, (n, n), 1)
    return jnp.sum(jnp.where(r == c, col, 0.0), axis=0, keepdims=True)


def _rms(x, g):
    return x * lax.rsqrt(jnp.mean(x * x, axis=-1, keepdims=True) + EPS) * g


def _rmsnorm_kernel(x_ref, g_ref, o_ref):
    o_ref[...] = _rms(x_ref[...].astype(F32), g_ref[...]).astype(o_ref.dtype)


def rmsnorm_rows(x, g, out_dtype):
    r, d = x.shape
    tr = _pick_tile(r, 256, 8)
    return pl.pallas_call(
        _rmsnorm_kernel,
        grid=(pl.cdiv(r, tr),),
        in_specs=[pl.BlockSpec((tr, d), lambda i: (i, 0)), pl.BlockSpec((1, d), lambda i: (0, 0))],
        out_specs=pl.BlockSpec((tr, d), lambda i: (i, 0)),
        out_shape=jax.ShapeDtypeStruct((r, d), out_dtype),
        compiler_params=_cparams(("parallel",)),
        name="rmsnorm_rows",
    )(x, g.reshape(1, d).astype(F32))


def _add_norm_kernel(x_ref, y_ref, gpost_ref, *rest, with_next):
    x_new = x_ref[...] + _rms(y_ref[...], gpost_ref[...])
    if with_next:
        gpre_ref, xo_ref, ho_ref = rest
        ho_ref[...] = _rms(x_new, gpre_ref[...]).astype(ho_ref.dtype)
    else:
        (xo_ref,) = rest
    xo_ref[...] = x_new


def add_norm(x, y, g_post, g_pre=None):
    r, d = x.shape
    tr = _pick_tile(r, 256, 8)
    row = pl.BlockSpec((tr, d), lambda i: (i, 0))
    vec = pl.BlockSpec((1, d), lambda i: (0, 0))
    with_next = g_pre is not None
    ins = [x, y, g_post.reshape(1, d).astype(F32)]
    in_specs = [row, row, vec]
    out_shape = [jax.ShapeDtypeStruct((r, d), F32)]
    out_specs = [row]
    if with_next:
        ins.append(g_pre.reshape(1, d).astype(F32))
        in_specs.append(vec)
        out_shape.append(jax.ShapeDtypeStruct((r, d), BF16))
        out_specs.append(row)
    outs = pl.pallas_call(
        functools.partial(_add_norm_kernel, with_next=with_next),
        grid=(pl.cdiv(r, tr),),
        in_specs=in_specs, out_specs=out_specs, out_shape=out_shape,
        compiler_params=_cparams(("parallel",)),
        name="add_norm",
    )(*ins)
    return (outs[0], outs[1]) if with_next else (outs[0], None)


def _mm_kernel(*refs, n_w, has_bias, nk, nm, tm, tail, cast_w, grouped):
    if grouped:
        te_ref, tf_ref, tv_ref = refs[:3]
        refs = refs[3:]
    x_ref = refs[0]
    w_refs = refs[1:1 + n_w]
    pos = 1 + n_w
    b_ref = None
    if has_bias:
        b_ref = refs[pos]
        pos += 1
    o_ref = refs[pos]
    pos += 1
    wb_refs = w_refs
    if cast_w:
        wb_refs = refs[pos:pos + n_w]
        pos += n_w
    acc_ref = refs[pos] if nk > 1 else None

    k = pl.program_id(1)
    m = pl.program_id(2)

    if cast_w:
        first = (tf_ref[m] == 1) if grouped else (m == 0)

        @pl.when(first)
        def _():
            for w_ref, wb_ref in zip(w_refs, wb_refs):
                wb_ref[...] = w_ref[...].astype(BF16)

    def finish(rows, vals):
        if n_w == 2:
            res = _silu(vals[0]) * vals[1]
        else:
            res = vals[0]
        if has_bias:
            res = res + b_ref[...]
        o_ref[0:rows, :] = res.astype(o_ref.dtype)

    def body(rows):
        x = x_ref[0:rows, :]
        vals = [jnp.dot(x, wb_ref[...], preferred_element_type=F32) for wb_ref in wb_refs]
        if nk == 1:
            finish(rows, vals)
            return

        @pl.when(k == 0)
        def _():
            acc_ref[m, 0:rows, :] = vals[0]

        @pl.when(jnp.logical_and(k > 0, k < nk - 1))
        def _():
            acc_ref[m, 0:rows, :] += vals[0]

        @pl.when(k == nk - 1)
        def _():
            finish(rows, [acc_ref[m, 0:rows, :] + vals[0]])

    if grouped:
        @pl.when(tv_ref[m] == 1)
        def _():
            body(tm)

        @pl.when(jnp.logical_and(tv_ref[m] == 0, k == nk - 1))
        def _():
            o_ref[...] = jnp.zeros_like(o_ref)
    elif tail == tm:
        body(tm)
    else:
        @pl.when(m < nm - 1)
        def _():
            body(tm)

        @pl.when(m == nm - 1)
        def _():
            body(tail)


def matmul(x, ws, bias=None, *, out_dtype=F32, tm=1024, tn=512, tk=None, groups=None, name="matmul"):
    ws = list(ws)
    n_w = len(ws)
    grouped = groups is not None
    r, kdim = x.shape
    ndim = ws[0].shape[-1]
    cast_w = ws[0].dtype != BF16
    tm = _pick_tile(r, tm, 8)
    if grouped:
        assert r % tm == 0
    nm = pl.cdiv(r, tm)
    tail = r - (nm - 1) * tm
    tn = _pick_tile(ndim, tn, 128)
    nn = pl.cdiv(ndim, tn)
    tk = kdim if tk is None else tk
    assert kdim % tk == 0
    nk = kdim // tk
    assert nk == 1 or n_w == 1

    if nk == 1:
        def out_row(k, m):
            return m
    else:
        def out_row(k, m):
            return jnp.where(k == nk - 1, m, 0)

    if grouped:
        x_map = lambda n, k, m, te, tf, tv: (m, k)
        w_map = lambda n, k, m, te, tf, tv: (te[m], k, n)
        b_map = lambda n, k, m, te, tf, tv: (0, n)
        o_map = lambda n, k, m, te, tf, tv: (out_row(k, m), n)
        w_spec = pl.BlockSpec((None, tk, tn), w_map)
    else:
        x_map = lambda n, k, m: (m, k)
        w_map = lambda n, k, m: (k, n)
        b_map = lambda n, k, m: (0, n)
        o_map = lambda n, k, m: (out_row(k, m), n)
        w_spec = pl.BlockSpec((tk, tn), w_map)

    in_specs = [pl.BlockSpec((tm, tk), x_map)] + [w_spec] * n_w
    ins = [x] + ws
    if bias is not None:
        in_specs.append(pl.BlockSpec((1, tn), b_map))
        ins.append(bias.reshape(1, ndim).astype(F32))
    scratch = []
    if cast_w:
        scratch += [pltpu.VMEM((tk, tn), BF16)] * n_w
    if nk > 1:
        scratch.append(pltpu.VMEM((nm, tm, tn), F32))
    kern = functools.partial(_mm_kernel, n_w=n_w, has_bias=bias is not None, nk=nk, nm=nm, tm=tm,
                             tail=tail, cast_w=cast_w, grouped=grouped)
    grid_spec = pltpu.PrefetchScalarGridSpec(
        num_scalar_prefetch=3 if grouped else 0,
        grid=(nn, nk, nm),
        in_specs=in_specs,
        out_specs=pl.BlockSpec((tm, tn), o_map),
        scratch_shapes=scratch,
    )
    call = pl.pallas_call(
        kern, grid_spec=grid_spec,
        out_shape=jax.ShapeDtypeStruct((r, ndim), out_dtype),
        compiler_params=_cparams(("arbitrary", "arbitrary", "arbitrary")),
        name=name,
    )
    if grouped:
        return call(*groups, *ins)
    return call(*ins)


def _mlstm_kernel(qz_ref, kz_ref, vz_ref, oz_ref, li_ref, lf_ref, cwq_ref, cwk_ref, cbq_ref, cbk_ref,
                  nw_ref, tq_ref, tk_ref, c0_ref, n0_ref, m0_ref,
                  h_ref, cn_ref, nn_ref, mn_ref,
                  prevq, prevk, c_s, n_s, m_s, *, L, nc, t_valid):
    c = pl.program_id(2)

    @pl.when(c == 0)
    def _():
        prevq[...] = jnp.zeros_like(prevq)
        prevk[...] = jnp.zeros_like(prevk)
        prevq[L - 8:L, :] = tq_ref[0]
        prevk[L - 8:L, :] = tk_ref[0]
        c_s[...] = c0_ref[0, 0]
        n_s[...] = n0_ref[0]
        m_s[...] = m0_ref[0]

    row = lax.broadcasted_iota(jnp.int32, (L, HEAD_DIM), 0)

    def conv(u, prev, w_ref, b_ref):
        out = b_ref[...] + w_ref[CONV_W - 1:CONV_W, :] * u
        for j in range(1, CONV_W):
            shifted = jnp.where(row < j, pltpu.roll(prev, j, 0), pltpu.roll(u, j, 0))
            out = out + w_ref[CONV_W - 1 - j:CONV_W - j, :] * shifted
        return out

    uq = qz_ref[...]
    uk = kz_ref[...]
    q = _silu(conv(uq, prevq[...], cwq_ref, cbq_ref))
    k = _silu(conv(uk, prevk[...], cwk_ref, cbk_ref)) * (HEAD_DIM ** -0.5)
    prevq[...] = uq
    prevk[...] = uk
    v = vz_ref[...]

    li_row = li_ref[0]
    lf_row = _log_sigmoid(lf_ref[0])
    if t_valid is not None:
        lane_t = c * L + lax.broadcasted_iota(jnp.int32, (1, L), 1)
        li_row = jnp.where(lane_t < t_valid, li_row, NEG_INF)
        lf_row = jnp.where(lane_t < t_valid, lf_row, 0.0)

    ri = lax.broadcasted_iota(jnp.int32, (L, L), 0)
    ci = lax.broadcasted_iota(jnp.int32, (L, L), 1)
    causal = ci <= ri
    b_col = jnp.sum(jnp.where(causal, lf_row, 0.0), axis=1, keepdims=True)
    b_row = _col_to_row(b_col, L)
    li_col = _row_to_col(li_row, L)
    m_prev = m_s[...]
    c_prev = c_s[...]
    n_prev = n_s[...]

    logw = jnp.where(causal, b_col - b_row + li_row, NEG_INF)
    g_col = b_col + m_prev
    m_t = jnp.maximum(jnp.max(logw, axis=1, keepdims=True), g_col)
    qb = q.astype(BF16)
    scores = _dot_nt(qb, k.astype(BF16)) * jnp.exp(logw - m_t)
    inter = jnp.exp(g_col - m_t)
    vb = v.astype(BF16)
    num = (jnp.dot(scores.astype(BF16), vb, preferred_element_type=F32)
           + inter * jnp.dot(qb, c_prev.astype(BF16), preferred_element_type=F32))
    den = jnp.sum(scores, axis=1, keepdims=True) + inter * jnp.sum(q * n_prev, axis=1, keepdims=True)
    h = num / jnp.maximum(jnp.abs(den), jnp.exp(-m_t))

    m_new = m_t[L - 1:L, :]
    b_last = b_col[L - 1:L, :]
    wk_col = jnp.exp(b_last - b_col + li_col - m_new)
    decay = jnp.exp(b_last + m_prev - m_new)
    kw = k * wk_col
    c_new = decay * c_prev + jnp.dot(kw.T.astype(BF16), vb, preferred_element_type=F32)
    n_new = decay * n_prev + jnp.sum(kw, axis=0, keepdims=True)
    c_s[...] = c_new
    n_s[...] = n_new
    m_s[...] = m_new

    hn = _rms(h, nw_ref[...])
    h_ref[...] = (hn * _sigmoid(oz_ref[...])).astype(h_ref.dtype)

    @pl.when(c == nc - 1)
    def _():
        cn_ref[0, 0] = c_new
        nn_ref[0] = n_new
        mn_ref[0] = m_new


def mlstm_group(z, gates_t, conv_w, conv_b, norm_w, conv_tail, c0, n0, m0, *, bsz, t, n_heads, t_valid):
    L = 128
    nc = t // L
    gpb = gates_t.shape[0] // bsz
    h = n_heads
    zspec = lambda off: pl.BlockSpec((L, HEAD_DIM), lambda b, hh, c: (b * nc + c, off + hh))
    gspec = lambda off: pl.BlockSpec((1, 1, L), lambda b, hh, c: (b * gpb + off + hh, 0, c))
    wspec = lambda off: pl.BlockSpec((CONV_W, HEAD_DIM), lambda b, hh, c: (0, off + hh))
    bspec = lambda off: pl.BlockSpec((1, HEAD_DIM), lambda b, hh, c: (0, off + hh))
    tspec = lambda off: pl.BlockSpec((1, 8, HEAD_DIM), lambda b, hh, c: (b, 0, off + hh))
    in_specs = [zspec(0), zspec(h), zspec(2 * h), zspec(3 * h), gspec(0), gspec(h),
                wspec(0), wspec(h), bspec(0), bspec(h),
                pl.BlockSpec((1, HEAD_DIM), lambda b, hh, c: (0, 0)),
                tspec(0), tspec(h),
                pl.BlockSpec((1, 1, HEAD_DIM, HEAD_DIM), lambda b, hh, c: (b, hh, 0, 0)),
                pl.BlockSpec((1, 1, HEAD_DIM), lambda b, hh, c: (b * h + hh, 0, 0)),
                pl.BlockSpec((1, 1, 1), lambda b, hh, c: (b * h + hh, 0, 0))]
    out_specs = [pl.BlockSpec((L, HEAD_DIM), lambda b, hh, c: (b * nc + c, hh)),
                 pl.BlockSpec((1, 1, HEAD_DIM, HEAD_DIM), lambda b, hh, c: (b, hh, 0, 0)),
                 pl.BlockSpec((1, 1, HEAD_DIM), lambda b, hh, c: (b * h + hh, 0, 0)),
                 pl.BlockSpec((1, 1, 1), lambda b, hh, c: (b * h + hh, 0, 0))]
    out_shape = [jax.ShapeDtypeStruct((bsz * t, h * HEAD_DIM), BF16),
                 jax.ShapeDtypeStruct((bsz, h, HEAD_DIM, HEAD_DIM), F32),
                 jax.ShapeDtypeStruct((bsz * h, 1, HEAD_DIM), F32),
                 jax.ShapeDtypeStruct((bsz * h, 1, 1), F32)]
    scratch = [pltpu.VMEM((L, HEAD_DIM), F32), pltpu.VMEM((L, HEAD_DIM), F32),
               pltpu.VMEM((HEAD_DIM, HEAD_DIM), F32), pltpu.VMEM((1, HEAD_DIM), F32),
               pltpu.VMEM((1, 1), F32)]
    outs = pl.pallas_call(
        functools.partial(_mlstm_kernel, L=L, nc=nc, t_valid=t_valid),
        grid=(bsz, h, nc), in_specs=in_specs, out_specs=out_specs, out_shape=out_shape,
        scratch_shapes=scratch,
        compiler_params=_cparams(("parallel", "parallel", "arbitrary")),
        name="mlstm_group",
    )(z, z, z, z, gates_t, gates_t, conv_w, conv_w, conv_b, conv_b, norm_w.reshape(1, HEAD_DIM),
      conv_tail, conv_tail, c0, n0.reshape(bsz * h, 1, HEAD_DIM), m0.reshape(bsz * h, 1, 1))
    return (outs[0], outs[1], outs[2].reshape(bsz, h, HEAD_DIM), outs[3].reshape(bsz, h))


def _hgrn2_kernel(qz_ref, fz_ref, iz_ref, gz_ref, lb_ref, nw_ref, s0_ref, o_ref, sn_ref, s_s,
                  *, L, nc, t_valid):
    c = pl.program_id(2)

    @pl.when(c == 0)
    def _():
        s_s[...] = s0_ref[0, 0]

    lb = lb_ref[...]
    q = _silu(qz_ref[...])
    fg = lb + (1.0 - lb) * _sigmoid(fz_ref[...])
    k = 1.0 - fg
    lf = jnp.log(fg)
    v = iz_ref[...]
    if t_valid is not None:
        t_idx = c * L + lax.broadcasted_iota(jnp.int32, (L, HEAD_DIM), 0)
        lf = jnp.where(t_idx < t_valid, lf, 0.0)
        k = jnp.where(t_idx < t_valid, k, 0.0)

    ri = lax.broadcasted_iota(jnp.int32, (L, L), 0)
    ci = lax.broadcasted_iota(jnp.int32, (L, L), 1)
    tril = jnp.where(ci <= ri, 1.0, 0.0).astype(F32)
    b = jnp.dot(tril, lf, preferred_element_type=F32, precision=HIGHEST)
    s_prev = s_s[...]
    vb = v.astype(BF16)
    o_inter = jnp.dot((q * jnp.exp(b)).astype(BF16), s_prev.astype(BF16), preferred_element_type=F32)

    srow = lax.broadcasted_iota(jnp.int32, (L, HEAD_DIM), 0)
    trow = lax.broadcasted_iota(jnp.int32, (SUB, HEAD_DIM), 0)
    pieces = []
    for blk in range(L // SUB):
        r0 = blk * SUB
        q_i = q[r0:r0 + SUB]
        b_i = b[r0:r0 + SUB]
        o_i = o_inter[r0:r0 + SUB]
        if blk > 0:
            b_mid = b[r0 - 1:r0]
            qp = q_i * jnp.exp(b_i - b_mid)
            kp = k * jnp.exp(jnp.where(srow < r0, b_mid - b, NEG_INF))
            a = _dot_nt(qp.astype(BF16), kp.astype(BF16))
            o_i = o_i + jnp.dot(a.astype(BF16), vb, preferred_element_type=F32)
        for s in range(SUB):
            w = jnp.exp(jnp.where(trow >= s, b_i - b_i[s:s + 1], NEG_INF))
            a_col = jnp.sum(q_i * k[r0 + s:r0 + s + 1] * w, axis=1, keepdims=True)
            o_i = o_i + a_col * v[r0 + s:r0 + s + 1]
        pieces.append(o_i)
    o = jnp.concatenate(pieces, axis=0)

    b_last = b[L - 1:L]
    kw = k * jnp.exp(b_last - b)
    s_new = (_row_to_col(jnp.exp(b_last), HEAD_DIM) * s_prev
             + jnp.dot(kw.T.astype(BF16), vb, preferred_element_type=F32))
    s_s[...] = s_new

    o_ref[...] = (_rms(o, nw_ref[...]) * _silu(gz_ref[...])).astype(o_ref.dtype)

    @pl.when(c == nc - 1)
    def _():
        sn_ref[0, 0] = s_new


def hgrn2_group(z, col0, lb, norm_w, s0, *, bsz, t, n_heads, t_valid):
    L = 128
    nc = t // L
    h = n_heads
    zspec = lambda off: pl.BlockSpec((L, HEAD_DIM), lambda b, hh, c: (b * nc + c, col0 + off + hh))
    sspec = pl.BlockSpec((1, 1, HEAD_DIM, HEAD_DIM), lambda b, hh, c: (b, hh, 0, 0))
    outs = pl.pallas_call(
        functools.partial(_hgrn2_kernel, L=L, nc=nc, t_valid=t_valid),
        grid=(bsz, h, nc),
        in_specs=[zspec(0), zspec(h), zspec(2 * h), zspec(3 * h),
                  pl.BlockSpec((1, HEAD_DIM), lambda b, hh, c: (0, hh)),
                  pl.BlockSpec((1, HEAD_DIM), lambda b, hh, c: (0, 0)),
                  sspec],
        out_specs=[pl.BlockSpec((L, HEAD_DIM), lambda b, hh, c: (b * nc + c, hh)), sspec],
        out_shape=[jax.ShapeDtypeStruct((bsz * t, h * HEAD_DIM), BF16),
                   jax.ShapeDtypeStruct((bsz, h, HEAD_DIM, HEAD_DIM), F32)],
        scratch_shapes=[pltpu.VMEM((HEAD_DIM, HEAD_DIM), F32)],
        compiler_params=_cparams(("parallel", "parallel", "arbitrary")),
        name="hgrn2_group",
    )(z, z, z, z, lb, norm_w.reshape(1, HEAD_DIM), s0)
    return outs[0], outs[1]


def _fox_gate_kernel(g_ref, lf_ref, cum_ref, *, t):
    lf = _log_sigmoid(g_ref[0])
    lf_ref[0] = lf
    ri = lax.broadcasted_iota(jnp.int32, (128, 128), 0)
    ci = lax.broadcasted_iota(jnp.int32, (128, 128), 1)
    upper = jnp.where(ri <= ci, 1.0, 0.0).astype(F32)
    carry = jnp.zeros((lf.shape[0], 1), F32)
    for ch in range(t // 128):
        seg = lf[:, ch * 128:(ch + 1) * 128]
        cs = jnp.dot(seg, upper, preferred_element_type=F32, precision=HIGHEST) + carry
        cum_ref[0, :, ch * 128:(ch + 1) * 128] = cs
        carry = cs[:, 127:128]


def fox_gates(gates_t, row_block, n_heads):
    bsz, _, t = gates_t.shape
    spec = pl.BlockSpec((1, n_heads, t), lambda b: (b, 0, 0))
    return pl.pallas_call(
        functools.partial(_fox_gate_kernel, t=t),
        grid=(bsz,),
        in_specs=[pl.BlockSpec((1, n_heads, t), lambda b: (b, row_block, 0))],
        out_specs=[spec, spec],
        out_shape=[jax.ShapeDtypeStruct((bsz, n_heads, t), F32)] * 2,
        compiler_params=_cparams(("parallel",)),
        name="fox_gates",
    )(gates_t)


def _fox_prefill_kernel(q_ref, k_ref, v_ref, fq_ref, fk_ref, o_ref, *, tq, tk):
    qi = pl.program_id(2)
    q = q_ref[...].astype(BF16)
    fq_col = _row_to_col(fq_ref[0], tq)
    qpos = qi * tq + lax.broadcasted_iota(jnp.int32, (tq, tk), 0)
    kloc = lax.broadcasted_iota(jnp.int32, (tq, tk), 1)
    scale = HEAD_DIM ** -0.5

    def step(ki, carry):
        m, l, acc = carry
        start = pl.multiple_of(ki * tk, tk)
        kb = k_ref[pl.ds(start, tk), :].astype(BF16)
        vb = v_ref[pl.ds(start, tk), :].astype(BF16)
        s = _dot_nt(q, kb) * scale + fq_col - fk_ref[0, ki]
        s = jnp.where(ki * tk + kloc <= qpos, s, NEG_INF)
        m_new = jnp.maximum(m, jnp.max(s, axis=1, keepdims=True))
        corr = jnp.exp(m - m_new)
        p = jnp.exp(s - m_new)
        l = l * corr + jnp.sum(p, axis=1, keepdims=True)
        acc = acc * corr + jnp.dot(p.astype(BF16), vb, preferred_element_type=F32)
        return m_new, l, acc

    init = (jnp.full((tq, 1), NEG_INF, F32), jnp.zeros((tq, 1), F32), jnp.zeros((tq, HEAD_DIM), F32))
    n_blocks = (qi * tq + tq + tk - 1) // tk
    _, l, acc = lax.fori_loop(0, n_blocks, step, init)
    o_ref[...] = (acc / l).astype(o_ref.dtype)


def fox_prefill(z, col0, f_cum, *, bsz, t, n_heads):
    tq = min(256, t)
    tk = tq
    h = n_heads
    nq = t // tq
    fq = f_cum.reshape(bsz * h, 1, t)
    fk = f_cum.reshape(bsz * h, t // tk, 1, tk)
    return pl.pallas_call(
        functools.partial(_fox_prefill_kernel, tq=tq, tk=tk),
        grid=(bsz, h, nq),
        in_specs=[pl.BlockSpec((tq, HEAD_DIM), lambda b, hh, i: (b * nq + i, col0 + hh)),
                  pl.BlockSpec((t, HEAD_DIM), lambda b, hh, i: (b, col0 + h + hh)),
                  pl.BlockSpec((t, HEAD_DIM), lambda b, hh, i: (b, col0 + 2 * h + hh)),
                  pl.BlockSpec((1, 1, tq), lambda b, hh, i: (b * h + hh, 0, i)),
                  pl.BlockSpec((1, t // tk, 1, tk), lambda b, hh, i: (b * h + hh, 0, 0, 0))],
        out_specs=pl.BlockSpec((tq, HEAD_DIM), lambda b, hh, i: (b * nq + i, hh)),
        out_shape=jax.ShapeDtypeStruct((bsz * t, h * HEAD_DIM), BF16),
        compiler_params=_cparams(("parallel", "parallel", "arbitrary")),
        name="fox_prefill",
    )(z, z, z, fq, fk)


def _fox_decode_kernel(pt_ref, wq_ref, kp_ref, vp_ref, lfp_ref, gcol_ref, knew_ref, vnew_ref, gnew_ref,
                       o_ref, m_s, l_s, acc_s, carry_s, *, n_pages, n_heads, page):
    j = pl.program_id(1)
    rows = n_heads * 8
    scale = HEAD_DIM ** -0.5

    @pl.when(j == 0)
    def _():
        m_s[...] = jnp.full_like(m_s, NEG_INF)
        l_s[...] = jnp.zeros_like(l_s)
        acc_s[...] = jnp.zeros_like(acc_s)
        carry_s[...] = jnp.zeros_like(carry_s)

    wq = wq_ref[0]

    def update(s, vb):
        m_old = m_s[...]
        m_new = jnp.maximum(m_old, jnp.max(s, axis=1, keepdims=True))
        corr = jnp.exp(m_old - m_new)
        p = jnp.exp(s - m_new)
        l_s[...] = l_s[...] * corr + jnp.sum(p, axis=1, keepdims=True)
        acc_s[...] = acc_s[...] * corr + jnp.dot(p.astype(BF16), vb, preferred_element_type=F32)
        m_s[...] = m_new

    lf_t = lfp_ref[0]
    ri = lax.broadcasted_iota(jnp.int32, (page, page), 0)
    ci = lax.broadcasted_iota(jnp.int32, (page, page), 1)
    later = jnp.where(ri > ci, 1.0, 0.0).astype(F32)
    suffix = jnp.dot(lf_t, later, preferred_element_type=F32, precision=HIGHEST) + carry_s[...]
    suffix_rows = jnp.concatenate(
        [jnp.broadcast_to(suffix[hh:hh + 1, :], (8, page)) for hh in range(n_heads)], axis=0)
    s = _dot_nt(wq, kp_ref[0].astype(BF16)) * scale + gcol_ref[0] + suffix_rows
    update(s, vp_ref[0].astype(BF16))
    carry_s[...] = carry_s[...] + jnp.sum(lf_t, axis=1, keepdims=True)

    @pl.when(j == n_pages - 1)
    def _():
        s_new = _dot_nt(wq, knew_ref[0].astype(BF16)) * scale + gnew_ref[0]
        update(s_new, vnew_ref[0].astype(BF16))
        out = acc_s[...] / l_s[...]
        for hh in range(n_heads):
            o_ref[0, :, hh * HEAD_DIM:(hh + 1) * HEAD_DIM] = (
                out[hh * 8:(hh + 1) * 8, hh * HEAD_DIM:(hh + 1) * HEAD_DIM].astype(o_ref.dtype))


def fox_decode(wq, k_pool, v_pool, lf_pool_t, page_table, g_col, k_new, v_new, g_new, *, n_heads):
    bsz, n_pages = page_table.shape
    _, page, hd = k_pool.shape
    rows = n_heads * 8
    pool_map = lambda b, j, pt: (pt[b, n_pages - 1 - j], 0, 0)
    per_b = lambda b, j, pt: (b, 0, 0)
    grid_spec = pltpu.PrefetchScalarGridSpec(
        num_scalar_prefetch=1,
        grid=(bsz, n_pages),
        in_specs=[pl.BlockSpec((1, rows, hd), per_b),
                  pl.BlockSpec((1, page, hd), pool_map),
                  pl.BlockSpec((1, page, hd), pool_map),
                  pl.BlockSpec((1, n_heads, page), pool_map),
                  pl.BlockSpec((1, rows, 1), per_b),
                  pl.BlockSpec((1, page, hd), per_b),
                  pl.BlockSpec((1, page, hd), per_b),
                  pl.BlockSpec((1, rows, page), per_b)],
        out_specs=pl.BlockSpec((1, 8, hd), per_b),
        scratch_shapes=[pltpu.VMEM((rows, 1), F32), pltpu.VMEM((rows, 1), F32),
                        pltpu.VMEM((rows, hd), F32), pltpu.VMEM((n_heads, 1), F32)],
    )
    return pl.pallas_call(
        functools.partial(_fox_decode_kernel, n_pages=n_pages, n_heads=n_heads, page=page),
        grid_spec=grid_spec,
        out_shape=jax.ShapeDtypeStruct((bsz, 8, hd), BF16),
        compiler_params=_cparams(("parallel", "arbitrary")),
        name="fox_decode",
    )(page_table, wq, k_pool, v_pool, lf_pool_t, g_col, k_new, v_new, g_new)


def _cross_attn_kernel(q_ref, k_ref, v_ref, o_ref):
    s = _dot_nt(q_ref[0].astype(BF16), k_ref[0].astype(BF16)) * (HEAD_DIM ** -0.5)
    m = jnp.max(s, axis=1, keepdims=True)
    p = jnp.exp(s - m)
    o = jnp.dot(p.astype(BF16), v_ref[0].astype(BF16), preferred_element_type=F32)
    o_ref[0] = (o / jnp.sum(p, axis=1, keepdims=True)).astype(o_ref.dtype)


def cross_attn(q, mem_k, mem_v):
    bsz, t, w = q.shape
    n_mem = mem_k.shape[1]
    h = w // HEAD_DIM
    tq = _pick_tile(t, 512, 8)
    return pl.pallas_call(
        _cross_attn_kernel,
        grid=(bsz, h, t // tq),
        in_specs=[pl.BlockSpec((1, tq, HEAD_DIM), lambda b, hh, i: (b, i, hh)),
                  pl.BlockSpec((1, n_mem, HEAD_DIM), lambda b, hh, i: (b, 0, hh)),
                  pl.BlockSpec((1, n_mem, HEAD_DIM), lambda b, hh, i: (b, 0, hh))],
        out_specs=pl.BlockSpec((1, tq, HEAD_DIM), lambda b, hh, i: (b, i, hh)),
        out_shape=jax.ShapeDtypeStruct((bsz, t, w), BF16),
        compiler_params=_cparams(("parallel", "parallel", "parallel")),
        name="cross_attn",
    )(q, mem_k, mem_v)


def _router_kernel(x_ref, g_ref, wr_ref, br_ref, o_ref, *, n_experts):
    xn = _rms(x_ref[...], g_ref[...])
    logits = lax.dot_general(wr_ref[...], xn, (((1,), (1,)), ((), ())),
                             preferred_element_type=F32, precision=HIGHEST) + br_ref[...]
    e_idx = lax.broadcasted_iota(jnp.int32, logits.shape, 0)
    v1 = jnp.max(logits, axis=0, keepdims=True)
    i1 = jnp.min(jnp.where(logits == v1, e_idx, n_experts), axis=0, keepdims=True)
    rest = jnp.where(e_idx == i1, NEG_INF, logits)
    v2 = jnp.max(rest, axis=0, keepdims=True)
    i2 = jnp.min(jnp.where(rest == v2, e_idx, n_experts), axis=0, keepdims=True)
    e2 = jnp.exp(v2 - v1)
    g1 = 1.0 / (1.0 + e2)
    g2 = e2 / (1.0 + e2)
    zeros = jnp.zeros((4, logits.shape[1]), F32)
    o_ref[...] = jnp.concatenate([i1.astype(F32), i2.astype(F32), g1, g2, zeros], axis=0)


def route_top2(x, g_pre, w_router, b_router):
    r, d = x.shape
    n_experts = w_router.shape[1]
    tr = _pick_tile(r, 256, 128)
    return pl.pallas_call(
        functools.partial(_router_kernel, n_experts=n_experts),
        grid=(pl.cdiv(r, tr),),
        in_specs=[pl.BlockSpec((tr, d), lambda i: (i, 0)),
                  pl.BlockSpec((1, d), lambda i: (0, 0)),
                  pl.BlockSpec((n_experts, d), lambda i: (0, 0)),
                  pl.BlockSpec((n_experts, 1), lambda i: (0, 0))],
        out_specs=pl.BlockSpec((8, tr), lambda i: (0, i)),
        out_shape=jax.ShapeDtypeStruct((8, r), F32),
        compiler_params=_cparams(("parallel",)),
        name="route_top2",
    )(x, g_pre.reshape(1, d).astype(F32), w_router.T.astype(F32), b_router.reshape(n_experts, 1).astype(F32))


def _dispatch_kernel(src_ref, x_hbm, g_ref, o_ref, buf, sem, *, tg):
    base = pl.program_id(0) * tg

    def row_copy(j):
        return pltpu.make_async_copy(x_hbm.at[pl.ds(src_ref[base + j], 1)], buf.at[pl.ds(j, 1)], sem)

    def start(j, carry):
        row_copy(j).start()
        return carry

    def wait(j, carry):
        row_copy(j).wait()
        return carry

    lax.fori_loop(0, tg, start, 0)
    lax.fori_loop(0, tg, wait, 0)
    o_ref[...] = _rms(buf[...], g_ref[...]).astype(o_ref.dtype)


def dispatch_rows(x, g_pre, src_rows):
    r, d = x.shape
    n_out = src_rows.shape[0]
    tg = 256
    assert n_out % tg == 0
    grid_spec = pltpu.PrefetchScalarGridSpec(
        num_scalar_prefetch=1,
        grid=(n_out // tg,),
        in_specs=[pl.BlockSpec(memory_space=pl.ANY), pl.BlockSpec((1, d), lambda i, src: (0, 0))],
        out_specs=pl.BlockSpec((tg, d), lambda i, src: (i, 0)),
        scratch_shapes=[pltpu.VMEM((tg, d), F32), pltpu.SemaphoreType.DMA(())],
    )
    return pl.pallas_call(
        functools.partial(_dispatch_kernel, tg=tg),
        grid_spec=grid_spec,
        out_shape=jax.ShapeDtypeStruct((n_out, d), BF16),
        compiler_params=_cparams(("arbitrary",)),
        name="dispatch_rows",
    )(src_rows, x, g_pre.reshape(1, d).astype(F32))


def _combine_kernel(p1_ref, p2_ref, y_hbm, g1_ref, g2_ref, o_ref, buf1, buf2, sem, *, tc):
    base = pl.program_id(0) * tc

    def copies(j):
        return (pltpu.make_async_copy(y_hbm.at[pl.ds(p1_ref[base + j], 1)], buf1.at[pl.ds(j, 1)], sem.at[0]),
                pltpu.make_async_copy(y_hbm.at[pl.ds(p2_ref[base + j], 1)], buf2.at[pl.ds(j, 1)], sem.at[1]))

    def start(j, carry):
        for cp in copies(j):
            cp.start()
        return carry

    def wait(j, carry):
        for cp in copies(j):
            cp.wait()
        return carry

    lax.fori_loop(0, tc, start, 0)
    lax.fori_loop(0, tc, wait, 0)
    g1 = _row_to_col(g1_ref[0], tc)
    g2 = _row_to_col(g2_ref[0], tc)
    o_ref[...] = g1 * buf1[...] + g2 * buf2[...]


def combine_rows(y, pos1, pos2, gate1, gate2, n_rows):
    d = y.shape[1]
    tc = 128
    n_pad = pos1.shape[0]
    assert n_pad % tc == 0 and n_pad >= n_rows
    grid_spec = pltpu.PrefetchScalarGridSpec(
        num_scalar_prefetch=2,
        grid=(n_pad // tc,),
        in_specs=[pl.BlockSpec(memory_space=pl.ANY),
                  pl.BlockSpec((1, 1, tc), lambda i, p1, p2: (i, 0, 0)),
                  pl.BlockSpec((1, 1, tc), lambda i, p1, p2: (i, 0, 0))],
        out_specs=pl.BlockSpec((tc, d), lambda i, p1, p2: (i, 0)),
        scratch_shapes=[pltpu.VMEM((tc, d), F32), pltpu.VMEM((tc, d), F32), pltpu.SemaphoreType.DMA((2,))],
    )
    return pl.pallas_call(
        functools.partial(_combine_kernel, tc=tc),
        grid_spec=grid_spec,
        out_shape=jax.ShapeDtypeStruct((n_rows, d), F32),
        compiler_params=_cparams(("arbitrary",)),
        name="combine_rows",
    )(pos1, pos2, y, gate1.reshape(n_pad // tc, 1, tc), gate2.reshape(n_pad // tc, 1, tc))


MOE_TM = 512


def moe_ffn(x, g_pre, w_router, b_router, w1, w3, w2):
    r, d = x.shape
    n_experts = w1.shape[0]
    d_ff = w1.shape[2]
    tm = MOE_TM
    routed = route_top2(x, g_pre, w_router, b_router)
    idx = routed[0:2].astype(jnp.int32)
    gates = routed[2:4]

    e_flat = idx.reshape(-1)
    onehot = (e_flat[:, None] == jnp.arange(n_experts, dtype=jnp.int32)[None, :]).astype(jnp.int32)
    rank = jnp.sum((jnp.cumsum(onehot, axis=0) - onehot) * onehot, axis=1)
    counts = jnp.sum(onehot, axis=0)
    tiles = (counts + tm - 1) // tm
    tile_end = jnp.cumsum(tiles)
    tile_start = tile_end - tiles
    pos = (tile_start * tm)[e_flat] + rank
    n_tiles = (TOP_K * r + n_experts * (tm - 1)) // tm + 1
    n_tiles = _round_up(n_tiles * tm, 256) // tm
    token = jnp.tile(jnp.arange(r, dtype=jnp.int32), TOP_K)
    src_rows = jnp.zeros((n_tiles * tm,), jnp.int32).at[pos].set(token)
    t_idx = jnp.arange(n_tiles, dtype=jnp.int32)
    used = tile_end[-1]
    t_clamped = jnp.minimum(t_idx, used - 1)
    tile_expert = jnp.sum((t_clamped[:, None] >= tile_end[None, :]).astype(jnp.int32), axis=1)
    tile_valid = (t_idx < used).astype(jnp.int32)
    tile_first = jnp.logical_and(tile_valid == 1,
                                 jnp.any(t_idx[:, None] == tile_start[None, :], axis=1)).astype(jnp.int32)
    tile_first = tile_first.at[0].set(1)
    groups = (tile_expert.astype(jnp.int32), tile_first, tile_valid)

    xs = dispatch_rows(x, g_pre, src_rows)
    hid = matmul(xs, [w1, w3], out_dtype=BF16, tm=tm, tn=256, groups=groups, name="moe_up")
    y = matmul(hid, [w2], tm=tm, tn=256, tk=_div_tile(d_ff, 3584), groups=groups, name="moe_down")

    tc = 128
    n_pad = _round_up(r, tc)
    padi = lambda a: jnp.pad(a, (0, n_pad - r))
    return combine_rows(y, padi(pos[:r]), padi(pos[r:]), padi(gates[0]), padi(gates[1]), r)


def kernel(x_prompt, x_sample, cache_fox_k, cache_fox_v, cache_fox_logf, cache_mem_k, cache_mem_v,
           state_ml_conv, state_ml_C, state_ml_n, state_ml_m, state_hg_S, page_table, mem_prompt,
           w_in, b_in, ml_conv_w, ml_conv_b, ml_norm_w, hg_lower_bounds, hg_norm_w, w_out,
           ca_mem_norm, ca_w_q, ca_w_k, ca_w_v, ca_w_o,
           norm_mix_pre, norm_mix_post, norm_ca_pre, norm_ca_post, norm_ffn_pre, norm_ffn_post,
           ffn_w1, ffn_w3, ffn_w2, moe_router, moe_router_b, moe_w1, moe_w3, moe_w2):
    bp, seq, d_model = x_prompt.shape
    bs, dec_seq, _ = x_sample.shape
    depth = w_in.shape[0]
    ml_h = state_ml_C.shape[2]
    hg_h = state_hg_S.shape[2]
    fx_h = cache_fox_k.shape[3]
    ml_w, hg_w, fx_w = ml_h * HEAD_DIM, hg_h * HEAD_DIM, fx_h * HEAD_DIM
    n_pool, page = cache_fox_k.shape[1], cache_fox_k.shape[2]
    n_mem = mem_prompt.shape[1]
    ca_w = ca_w_q.shape[2]
    ca_h = ca_w // HEAD_DIM
    n_p, n_s = bp * seq, bs * dec_seq
    assert dec_seq <= 8 and seq % 256 == 0 or seq == 128

    sizes = (2 * ml_w, ml_w, ml_w, ml_h, ml_h, hg_w, hg_w, hg_w, hg_w, fx_w, fx_w, fx_w, fx_h)
    starts = [0]
    for s in sizes:
        starts.append(starts[-1] + s)
    wide = [(0, starts[3]), (starts[5], starts[12])]
    narrow = [(starts[3], starts[5]), (starts[12], starts[13])]
    n_wide = sum(b - a for a, b in wide)
    n_gate = sum(b - a for a, b in narrow)
    permute = lambda a: jnp.concatenate([a[..., s:e] for s, e in wide + narrow], axis=-1)
    hg_col = 4 * ml_h
    fx_col = hg_col + 4 * hg_h

    lb_soft = jax.nn.softmax(hg_lower_bounds.astype(F32), axis=0)
    lower_bounds = jnp.cumsum(lb_soft, axis=0) - lb_soft[0]

    x = jnp.concatenate([x_prompt.reshape(n_p, d_model), x_sample.reshape(n_s, d_model)], axis=0)
    h = rmsnorm_rows(x, norm_mix_pre[0], BF16)

    l_ml = 128
    acc = {name: [] for name in ('p_k', 'p_v', 'p_lf', 'p_mk', 'p_mv', 'p_conv', 'p_C', 'p_n', 'p_m', 'p_S',
                                 's_k', 's_v', 's_lf', 's_conv', 's_C', 's_n', 's_m', 's_S')}
    for l in range(depth):
        w_perm = permute(w_in[l]).astype(BF16)
        z = matmul(h, [w_perm], permute(b_in[l]), tm=1024, tn=1024, name="in_proj")
        zg = z[:, n_wide:n_wide + n_gate]
        z_p = z
        z_s = jnp.pad(z[n_p:].reshape(bs, dec_seq, -1), ((0, 0), (0, l_ml - dec_seq), (0, 0)))
        z_s = z_s.reshape(bs * l_ml, -1)
        g_p = zg[:n_p].reshape(bp, seq, n_gate).transpose(0, 2, 1)
        g_s = jnp.pad(zg[n_p:].reshape(bs, dec_seq, n_gate), ((0, 0), (0, l_ml - dec_seq), (0, 0)))
        g_s = g_s.transpose(0, 2, 1)

        qk_p = z[:n_p, :2 * ml_w].reshape(bp, seq, 2 * ml_w)
        qk_s = z[n_p:, :2 * ml_w].reshape(bs, dec_seq, 2 * ml_w)
        conv_p = qk_p[:, seq - (CONV_W - 1):]
        conv_s = jnp.concatenate([state_ml_conv[l], qk_s], axis=1)[:, dec_seq:]
        tail_p = jnp.zeros((bp, 8, 2 * ml_w), F32)
        tail_s = jnp.pad(state_ml_conv[l], ((0, 0), (8 - (CONV_W - 1), 0), (0, 0)))

        lb_l = lower_bounds[l].reshape(1, hg_w)
        hml_p, c_p, nrm_p, m_p = mlstm_group(
            z_p, g_p.reshape(bp * n_gate, 1, seq), ml_conv_w[l], ml_conv_b[l].reshape(1, -1), ml_norm_w[l],
            tail_p, jnp.zeros((bp, ml_h, HEAD_DIM, HEAD_DIM), F32), jnp.zeros((bp, ml_h, HEAD_DIM), F32),
            jnp.zeros((bp, ml_h), F32), bsz=bp, t=seq, n_heads=ml_h, t_valid=None)
        hml_s, c_s, nrm_s, m_s = mlstm_group(
            z_s, g_s.reshape(bs * n_gate, 1, l_ml), ml_conv_w[l], ml_conv_b[l].reshape(1, -1), ml_norm_w[l],
            tail_s, state_ml_C[l], state_ml_n[l], state_ml_m[l], bsz=bs, t=l_ml, n_heads=ml_h, t_valid=dec_seq)
        ohg_p, s_p = hgrn2_group(z_p, hg_col, lb_l, hg_norm_w[l], jnp.zeros((bp, hg_h, HEAD_DIM, HEAD_DIM), F32),
                                 bsz=bp, t=seq, n_heads=hg_h, t_valid=None)
        ohg_s, s_s = hgrn2_group(z_s, hg_col, lb_l, hg_norm_w[l], state_hg_S[l],
                                 bsz=bs, t=l_ml, n_heads=hg_h, t_valid=dec_seq)
        fx_row_block = (2 * ml_h) // fx_h
        assert fx_row_block * fx_h == 2 * ml_h
        lf_p, fcum_p = fox_gates(g_p, fx_row_block, fx_h)
        lf_s, fcum_s = fox_gates(g_s, fx_row_block, fx_h)
        ofx_p = fox_prefill(z_p, fx_col, fcum_p, bsz=bp, t=seq, n_heads=fx_h)

        fx0 = fx_col * HEAD_DIM
        q_s = z[n_p:, fx0:fx0 + fx_w].reshape(bs, dec_seq, fx_h, HEAD_DIM)
        k_s = z[n_p:, fx0 + fx_w:fx0 + 2 * fx_w].reshape(bs, dec_seq, fx_w)
        v_s = z[n_p:, fx0 + 2 * fx_w:fx0 + 3 * fx_w].reshape(bs, dec_seq, fx_w)
        eye_h = jnp.eye(fx_h, dtype=F32)
        wq = jnp.einsum('bthd,hg->bhtgd', q_s, eye_h)
        wq = jnp.pad(wq, ((0, 0), (0, 0), (0, 8 - dec_seq), (0, 0), (0, 0)))
        wq = wq.reshape(bs, fx_h * 8, fx_w).astype(BF16)
        g_tok = fcum_s[:, :, :8]
        tok = jnp.arange(8)
        g_col = jnp.where(tok[None, None, :] < dec_seq, g_tok, 0.0).reshape(bs, fx_h * 8, 1)
        src = jnp.arange(page)
        allowed = (src[None, :] <= tok[:, None]) & (src[None, :] < dec_seq) & (tok[:, None] < dec_seq)
        g_src = jnp.pad(g_tok, ((0, 0), (0, 0), (0, page - 8)))
        g_new = jnp.where(allowed[None, None], g_tok[:, :, :, None] - g_src[:, :, None, :], NEG_INF)
        g_new = jnp.where((tok[:, None] >= dec_seq) & (src[None, :] == 0), 0.0, g_new)
        g_new = g_new.reshape(bs, fx_h * 8, page)
        padk = lambda a: jnp.pad(a, ((0, 0), (0, page - dec_seq), (0, 0)))
        ofx_s = fox_decode(wq, cache_fox_k[l].reshape(n_pool, page, fx_w), cache_fox_v[l].reshape(n_pool, page, fx_w),
                           cache_fox_logf[l].transpose(0, 2, 1), page_table, g_col, padk(k_s), padk(v_s), g_new,
                           n_heads=fx_h)
        ofx_s = ofx_s[:, :dec_seq].reshape(n_s, fx_w)

        unpad = lambda a: a.reshape(bs, l_ml, -1)[:, :dec_seq].reshape(n_s, -1)
        merged = jnp.concatenate([
            jnp.concatenate([hml_p, ohg_p, ofx_p], axis=1),
            jnp.concatenate([unpad(hml_s), unpad(ohg_s), ofx_s], axis=1)], axis=0)
        mix = matmul(merged, [w_out[l]], tm=1024, tn=512, name="out_proj")
        x, h_ca = add_norm(x, mix, norm_mix_post[l], norm_ca_pre[l])

        hm = rmsnorm_rows(mem_prompt.reshape(bp * n_mem, d_model), ca_mem_norm[l], BF16)
        mk = matmul(hm, [ca_w_k[l]], tm=1024, tn=512, name="mem_k")
        mv = matmul(hm, [ca_w_v[l]], tm=1024, tn=512, name="mem_v")
        qc = matmul(h_ca, [ca_w_q[l]], tm=1024, tn=512, name="ca_q")
        o_p = cross_attn(qc[:n_p].reshape(bp, seq, ca_w), mk.reshape(bp, n_mem, ca_w), mv.reshape(bp, n_mem, ca_w))
        qc_s = jnp.pad(qc[n_p:].reshape(bs, dec_seq, ca_w), ((0, 0), (0, 8 - dec_seq), (0, 0)))
        o_s = cross_attn(qc_s, cache_mem_k[l].reshape(bs, n_mem, ca_w), cache_mem_v[l].reshape(bs, n_mem, ca_w))
        o_ca = jnp.concatenate([o_p.reshape(n_p, ca_w), o_s[:, :dec_seq].reshape(n_s, ca_w)], axis=0)
        c_out = matmul(o_ca, [ca_w_o[l]], tm=1024, tn=1024, name="ca_o")

        j = l // 2
        g_next = norm_mix_pre[l + 1] if l + 1 < depth else None
        if l % 2 == 0:
            x, h_ffn = add_norm(x, c_out, norm_ca_post[l], norm_ffn_pre[l])
            d_ff = ffn_w1.shape[2]
            hid = matmul(h_ffn, [ffn_w1[j], ffn_w3[j]], out_dtype=BF16, tm=1024, tn=256, name="ffn_up")
            f_out = matmul(hid, [ffn_w2[j]], tm=1024, tn=512, tk=_div_tile(d_ff, 2048), name="ffn_down")
        else:
            x, _ = add_norm(x, c_out, norm_ca_post[l])
            f_out = moe_ffn(x, norm_ffn_pre[l], moe_router[j], moe_router_b[j], moe_w1[j], moe_w3[j], moe_w2[j])
        x, h = add_norm(x, f_out, norm_ffn_post[l], g_next)

        k0, v0 = fx0 + fx_w, fx0 + 2 * fx_w
        acc['p_k'].append(z[:n_p, k0:k0 + fx_w].reshape(bp, seq, fx_h, HEAD_DIM))
        acc['p_v'].append(z[:n_p, v0:v0 + fx_w].reshape(bp, seq, fx_h, HEAD_DIM))
        acc['p_lf'].append(lf_p.transpose(0, 2, 1))
        acc['p_mk'].append(mk.reshape(bp, n_mem, ca_h, HEAD_DIM))
        acc['p_mv'].append(mv.reshape(bp, n_mem, ca_h, HEAD_DIM))
        acc['p_conv'].append(conv_p)
        acc['p_C'].append(c_p)
        acc['p_n'].append(nrm_p)
        acc['p_m'].append(m_p)
        acc['p_S'].append(s_p)
        acc['s_k'].append(k_s.reshape(bs, dec_seq, fx_h, HEAD_DIM))
        acc['s_v'].append(v_s.reshape(bs, dec_seq, fx_h, HEAD_DIM))
        acc['s_lf'].append(lf_s[:, :, :dec_seq].transpose(0, 2, 1))
        acc['s_conv'].append(conv_s)
        acc['s_C'].append(c_s)
        acc['s_n'].append(nrm_s)
        acc['s_m'].append(m_s)
        acc['s_S'].append(s_s)

    y_p = x[:n_p].reshape(bp, seq, d_model)
    y_s = x[n_p:].reshape(bs, dec_seq, d_model)
    order = ('p_k', 'p_v', 'p_lf', 'p_mk', 'p_mv', 'p_conv', 'p_C', 'p_n', 'p_m', 'p_S',
             's_k', 's_v', 's_lf', 's_conv', 's_C', 's_n', 's_m', 's_S')
    return (y_p, y_s) + tuple(jnp.stack(acc[name]) for name in order)
```

```python
import functools

import jax
import jax.numpy as jnp
from jax import lax
from jax.experimental import pallas as pl
from jax.experimental.pallas import tpu as pltpu

F32 = jnp.float32
BF16 = jnp.bfloat16
EPS = 1e-6
HEAD_DIM = 128
CONV_W = 4
TOP_K = 2
SUB = 16
VMEM_LIMIT_BYTES = 56 * 1024 * 1024
HIGHEST = lax.Precision.HIGHEST
NEG_INF = float("-inf")


def _cparams(sem):
    return pltpu.CompilerParams(dimension_semantics=sem, vmem_limit_bytes=VMEM_LIMIT_BYTES)


def _round_up(a, b):
    return (a + b - 1) // b * b


def _pick_tile(n, target, quantum):
    if n <= quantum:
        return n
    return max(quantum, min(target, n) // quantum * quantum)


def _div_tile(n, target):
    if n <= target:
        return n
    t = target // 128 * 128
    while t > 128 and n % t:
        t -= 128
    assert n % t == 0, (n, target)
    return t


def _log_sigmoid(x):
    return jnp.minimum(x, 0.0) - jnp.log(1.0 + jnp.exp(-jnp.abs(x)))


def _sigmoid(x):
    return 1.0 / (1.0 + jnp.exp(-x))


def _silu(x):
    return x * _sigmoid(x)


def _dot_nt(a, b):
    return lax.dot_general(a, b, (((1,), (1,)), ((), ())), preferred_element_type=F32)


def _row_to_col(row, n):
    r = lax.broadcasted_iota(jnp.int32, (n, n), 0)
    c = lax.broadcasted_iota(jnp.int32, (n, n), 1)
    return jnp.sum(jnp.where(r == c, row, 0.0), axis=1, keepdims=True)


def _col_to_row(col, n):
    r = lax.broadcasted_iota(jnp.int32, (n, n), 0)
    c = lax.broadcasted_iota(jnp.int32, (n, n), 1)
    return jnp.sum(jnp.where(r == c, col, 0.0), axis=0, keepdims=True)


def _rms(x, g):
    return x * lax.rsqrt(jnp.mean(x * x, axis=-1, keepdims=True) + EPS) * g


def _rmsnorm_kernel(x_ref, g_ref, o_ref):
    o_ref[...] = _rms(x_ref[...].astype(F32), g_ref[...]).astype(o_ref.dtype)


def rmsnorm_rows(x, g, out_dtype):
    r, d = x.shape
    tr = _pick_tile(r, 256, 8)
    return pl.pallas_call(
        _rmsnorm_kernel,
        grid=(pl.cdiv(r, tr),),
        in_specs=[pl.BlockSpec((tr, d), lambda i: (i, 0)), pl.BlockSpec((1, d), lambda i: (0, 0))],
        out_specs=pl.BlockSpec((tr, d), lambda i: (i, 0)),
        out_shape=jax.ShapeDtypeStruct((r, d), out_dtype),
        compiler_params=_cparams(("parallel",)),
        name="rmsnorm_rows",
    )(x, g.reshape(1, d).astype(F32))


def _add_norm_kernel(x_ref, y_ref, gpost_ref, *rest, with_next):
    x_new = x_ref[...] + _rms(y_ref[...], gpost_ref[...])
    if with_next:
        gpre_ref, xo_ref, ho_ref = rest
        ho_ref[...] = _rms(x_new, gpre_ref[...]).astype(ho_ref.dtype)
    else:
        (xo_ref,) = rest
    xo_ref[...] = x_new


def add_norm(x, y, g_post, g_pre=None):
    r, d = x.shape
    tr = _pick_tile(r, 256, 8)
    row = pl.BlockSpec((tr, d), lambda i: (i, 0))
    vec = pl.BlockSpec((1, d), lambda i: (0, 0))
    with_next = g_pre is not None
    ins = [x, y, g_post.reshape(1, d).astype(F32)]
    in_specs = [row, row, vec]
    out_shape = [jax.ShapeDtypeStruct((r, d), F32)]
    out_specs = [row]
    if with_next:
        ins.append(g_pre.reshape(1, d).astype(F32))
        in_specs.append(vec)
        out_shape.append(jax.ShapeDtypeStruct((r, d), BF16))
        out_specs.append(row)
    outs = pl.pallas_call(
        functools.partial(_add_norm_kernel, with_next=with_next),
        grid=(pl.cdiv(r, tr),),
        in_specs=in_specs, out_specs=out_specs, out_shape=out_shape,
        compiler_params=_cparams(("parallel",)),
        name="add_norm",
    )(*ins)
    return (outs[0], outs[1]) if with_next else (outs[0], None)


def _mm_kernel(*refs, n_w, has_bias, nk, nm, tm, tail, cast_w, grouped, shift, tn):
    if grouped:
        te_ref, tf_ref, tv_ref = refs[:3]
        refs = refs[3:]
    x_ref = refs[0]
    pos = 1
    w_refs = []
    for _ in range(n_w):
        if shift:
            w_refs.append((refs[pos], refs[pos + 1]))
            pos += 2
        else:
            w_refs.append((refs[pos],))
            pos += 1
    b_ref = None
    if has_bias:
        b_ref = refs[pos]
        pos += 1
    o_ref = refs[pos]
    pos += 1
    if cast_w:
        wb_refs = refs[pos:pos + n_w]
        pos += n_w
    else:
        wb_refs = [w[0] for w in w_refs]
    acc_ref = refs[pos] if nk > 1 else None

    k = pl.program_id(2)
    ml = pl.program_id(3)
    m = pl.program_id(0) * nm + ml

    if cast_w:
        first = jnp.logical_or(tf_ref[m] == 1, ml == 0) if grouped else (ml == 0)

        @pl.when(first)
        def _():
            for w, wb_ref in zip(w_refs, wb_refs):
                if shift:
                    wide = jnp.concatenate([w[0][...], w[1][...]], axis=1)
                    wb_ref[...] = wide[:, shift:shift + tn].astype(BF16)
                else:
                    wb_ref[...] = w[0][...].astype(BF16)

    def finish(rows, vals):
        if n_w == 2:
            res = _silu(vals[0]) * vals[1]
        else:
            res = vals[0]
        if has_bias:
            res = res + b_ref[...]
        o_ref[0:rows, :] = res.astype(o_ref.dtype)

    def body(rows):
        x = x_ref[0:rows, :]
        vals = [jnp.dot(x, wb_ref[...], preferred_element_type=F32) for wb_ref in wb_refs]
        if nk == 1:
            finish(rows, vals)
            return

        @pl.when(k == 0)
        def _():
            acc_ref[ml, 0:rows, :] = vals[0]

        @pl.when(jnp.logical_and(k > 0, k < nk - 1))
        def _():
            acc_ref[ml, 0:rows, :] += vals[0]

        @pl.when(k == nk - 1)
        def _():
            finish(rows, [acc_ref[ml, 0:rows, :] + vals[0]])

    if grouped:
        @pl.when(tv_ref[m] == 1)
        def _():
            body(tm)

        @pl.when(jnp.logical_and(tv_ref[m] == 0, k == nk - 1))
        def _():
            o_ref[...] = jnp.zeros_like(o_ref)
    elif tail == tm:
        body(tm)
    else:
        @pl.when(ml < nm - 1)
        def _():
            body(tm)

        @pl.when(ml == nm - 1)
        def _():
            body(tail)


def matmul(x, ws, bias=None, *, out_dtype=F32, tm=1024, tn=512, tk=None, groups=None, lead=None,
           col0=0, n_cols=None, row_splits=1, name="matmul"):
    ws = list(ws)
    n_w = len(ws)
    grouped = groups is not None
    three_d = ws[0].ndim == 3
    assert three_d == (grouped or lead is not None)
    r, kdim = x.shape
    n_total = ws[0].shape[-1]
    ndim = n_total - col0 if n_cols is None else n_cols
    cast_w = ws[0].dtype != BF16
    tm = _pick_tile(r, tm, 8)
    nm_all = pl.cdiv(r, tm)
    assert nm_all % row_splits == 0 and (row_splits == 1 or r % tm == 0)
    nm = nm_all // row_splits
    tail = r - (nm_all - 1) * tm
    if grouped:
        assert tail == tm
    tn = _pick_tile(ndim, tn, 128)
    nn = pl.cdiv(ndim, tn)
    shift = col0 % 128
    base = col0 - shift
    assert base % tn == 0 and (shift == 0 or (cast_w and tn % 128 == 0 and ndim % tn == 0))
    tk = kdim if tk is None else tk
    assert kdim % tk == 0
    nk = kdim // tk
    assert nk == 1 or n_w == 1

    def row_of(s, k, ml):
        m = s * nm + ml
        return m if nk == 1 else jnp.where(k == nk - 1, m, s * nm)

    def widx(args, blk, col_block):
        s, n, k, ml = args[:4]
        if grouped:
            return (args[4][s * nm + ml], k, col_block(n))
        if lead is not None:
            return (lead, k, col_block(n))
        return (k, col_block(n))

    wshape = lambda width: ((None, tk, width) if three_d else (tk, width))
    w_specs = [pl.BlockSpec(wshape(tn), lambda *a: widx(a, tn, lambda n: base // tn + n))]
    if shift:
        w_specs.append(pl.BlockSpec(wshape(128), lambda *a: widx(a, 128, lambda n: (base + (n + 1) * tn) // 128)))

    in_specs = [pl.BlockSpec((tm, tk), lambda *a: (a[0] * nm + a[3], a[2]))]
    ins = [x]
    for w in ws:
        in_specs += w_specs
        ins += [w] * len(w_specs)
    if bias is not None:
        in_specs.append(pl.BlockSpec((1, tn), lambda *a: (0, a[1])))
        ins.append(bias.reshape(1, ndim).astype(F32))
    scratch = []
    if cast_w:
        scratch += [pltpu.VMEM((tk, tn), BF16)] * n_w
    if nk > 1:
        scratch.append(pltpu.VMEM((nm, tm, tn), F32))
    kern = functools.partial(_mm_kernel, n_w=n_w, has_bias=bias is not None, nk=nk, nm=nm, tm=tm,
                             tail=tail, cast_w=cast_w, grouped=grouped, shift=shift, tn=tn)
    grid_spec = pltpu.PrefetchScalarGridSpec(
        num_scalar_prefetch=3 if grouped else 0,
        grid=(row_splits, nn, nk, nm),
        in_specs=in_specs,
        out_specs=pl.BlockSpec((tm, tn), lambda *a: (row_of(a[0], a[2], a[3]), a[1])),
        scratch_shapes=scratch,
    )
    call = pl.pallas_call(
        kern, grid_spec=grid_spec,
        out_shape=jax.ShapeDtypeStruct((r, ndim), out_dtype),
        compiler_params=_cparams(("arbitrary",) * 4),
        name=name,
    )
    if grouped:
        return call(*groups, *ins)
    return call(*ins)


def _mlstm_kernel(qz_ref, kz_ref, vz_ref, oz_ref, li_ref, lf_ref, cwq_ref, cwk_ref, cbq_ref, cbk_ref,
                  nw_ref, tq_ref, tk_ref, c0_ref, n0_ref, m0_ref,
                  h_ref, cn_ref, nn_ref, mn_ref,
                  prevq, prevk, c_s, n_s, m_s, *, L, nc, t_valid):
    c = pl.program_id(2)

    @pl.when(c == 0)
    def _():
        prevq[...] = jnp.zeros_like(prevq)
        prevk[...] = jnp.zeros_like(prevk)
        prevq[L - 8:L, :] = tq_ref[0]
        prevk[L - 8:L, :] = tk_ref[0]
        c_s[...] = c0_ref[0, 0]
        n_s[...] = n0_ref[0]
        m_s[...] = m0_ref[0]

    row = lax.broadcasted_iota(jnp.int32, (L, HEAD_DIM), 0)

    def conv(u, prev, w_ref, b_ref):
        out = b_ref[...] + w_ref[CONV_W - 1:CONV_W, :] * u
        for j in range(1, CONV_W):
            shifted = jnp.where(row < j, pltpu.roll(prev, j, 0), pltpu.roll(u, j, 0))
            out = out + w_ref[CONV_W - 1 - j:CONV_W - j, :] * shifted
        return out

    uq = qz_ref[...]
    uk = kz_ref[...]
    q = _silu(conv(uq, prevq[...], cwq_ref, cbq_ref))
    k = _silu(conv(uk, prevk[...], cwk_ref, cbk_ref)) * (HEAD_DIM ** -0.5)
    prevq[...] = uq
    prevk[...] = uk
    v = vz_ref[...]

    li_row = li_ref[0]
    lf_row = _log_sigmoid(lf_ref[0])
    if t_valid is not None:
        lane_t = c * L + lax.broadcasted_iota(jnp.int32, (1, L), 1)
        li_row = jnp.where(lane_t < t_valid, li_row, NEG_INF)
        lf_row = jnp.where(lane_t < t_valid, lf_row, 0.0)

    ri = lax.broadcasted_iota(jnp.int32, (L, L), 0)
    ci = lax.broadcasted_iota(jnp.int32, (L, L), 1)
    causal = ci <= ri
    b_col = jnp.sum(jnp.where(causal, lf_row, 0.0), axis=1, keepdims=True)
    b_row = _col_to_row(b_col, L)
    li_col = _row_to_col(li_row, L)
    m_prev = m_s[...]
    c_prev = c_s[...]
    n_prev = n_s[...]

    logw = jnp.where(causal, b_col - b_row + li_row, NEG_INF)
    g_col = b_col + m_prev
    m_t = jnp.maximum(jnp.max(logw, axis=1, keepdims=True), g_col)
    qb = q.astype(BF16)
    scores = _dot_nt(qb, k.astype(BF16)) * jnp.exp(logw - m_t)
    inter = jnp.exp(g_col - m_t)
    vb = v.astype(BF16)
    num = (jnp.dot(scores.astype(BF16), vb, preferred_element_type=F32)
           + inter * jnp.dot(qb, c_prev.astype(BF16), preferred_element_type=F32))
    den = jnp.sum(scores, axis=1, keepdims=True) + inter * jnp.sum(q * n_prev, axis=1, keepdims=True)
    h = num / jnp.maximum(jnp.abs(den), jnp.exp(-m_t))

    m_new = m_t[L - 1:L, :]
    b_last = b_col[L - 1:L, :]
    wk_col = jnp.exp(b_last - b_col + li_col - m_new)
    decay = jnp.exp(b_last + m_prev - m_new)
    kw = k * wk_col
    c_new = decay * c_prev + jnp.dot(kw.T.astype(BF16), vb, preferred_element_type=F32)
    n_new = decay * n_prev + jnp.sum(kw, axis=0, keepdims=True)
    c_s[...] = c_new
    n_s[...] = n_new
    m_s[...] = m_new

    hn = _rms(h, nw_ref[...])
    h_ref[...] = (hn * _sigmoid(oz_ref[...])).astype(h_ref.dtype)

    @pl.when(c == nc - 1)
    def _():
        cn_ref[0, 0] = c_new
        nn_ref[0] = n_new
        mn_ref[0] = m_new


def mlstm_group(z, gates_t, conv_w, conv_b, norm_w, conv_tail, c0, n0, m0, *, bsz, t, n_heads, t_valid):
    L = 128
    nc = t // L
    gpb = gates_t.shape[0] // bsz
    h = n_heads
    zspec = lambda off: pl.BlockSpec((L, HEAD_DIM), lambda b, hh, c: (b * nc + c, off + hh))
    gspec = lambda off: pl.BlockSpec((1, 1, L), lambda b, hh, c: (b * gpb + off + hh, 0, c))
    wspec = lambda off: pl.BlockSpec((CONV_W, HEAD_DIM), lambda b, hh, c: (0, off + hh))
    bspec = lambda off: pl.BlockSpec((1, HEAD_DIM), lambda b, hh, c: (0, off + hh))
    tspec = lambda off: pl.BlockSpec((1, 8, HEAD_DIM), lambda b, hh, c: (b, 0, off + hh))
    in_specs = [zspec(0), zspec(h), zspec(2 * h), zspec(3 * h), gspec(0), gspec(h),
                wspec(0), wspec(h), bspec(0), bspec(h),
                pl.BlockSpec((1, HEAD_DIM), lambda b, hh, c: (0, 0)),
                tspec(0), tspec(h),
                pl.BlockSpec((1, 1, HEAD_DIM, HEAD_DIM), lambda b, hh, c: (b, hh, 0, 0)),
                pl.BlockSpec((1, 1, HEAD_DIM), lambda b, hh, c: (b * h + hh, 0, 0)),
                pl.BlockSpec((1, 1, 1), lambda b, hh, c: (b * h + hh, 0, 0))]
    out_specs = [pl.BlockSpec((L, HEAD_DIM), lambda b, hh, c: (b * nc + c, hh)),
                 pl.BlockSpec((1, 1, HEAD_DIM, HEAD_DIM), lambda b, hh, c: (b, hh, 0, 0)),
                 pl.BlockSpec((1, 1, HEAD_DIM), lambda b, hh, c: (b * h + hh, 0, 0)),
                 pl.BlockSpec((1, 1, 1), lambda b, hh, c: (b * h + hh, 0, 0))]
    out_shape = [jax.ShapeDtypeStruct((bsz * t, h * HEAD_DIM), BF16),
                 jax.ShapeDtypeStruct((bsz, h, HEAD_DIM, HEAD_DIM), F32),
                 jax.ShapeDtypeStruct((bsz * h, 1, HEAD_DIM), F32),
                 jax.ShapeDtypeStruct((bsz * h, 1, 1), F32)]
    scratch = [pltpu.VMEM((L, HEAD_DIM), F32), pltpu.VMEM((L, HEAD_DIM), F32),
               pltpu.VMEM((HEAD_DIM, HEAD_DIM), F32), pltpu.VMEM((1, HEAD_DIM), F32),
               pltpu.VMEM((1, 1), F32)]
    outs = pl.pallas_call(
        functools.partial(_mlstm_kernel, L=L, nc=nc, t_valid=t_valid),
        grid=(bsz, h, nc), in_specs=in_specs, out_specs=out_specs, out_shape=out_shape,
        scratch_shapes=scratch,
        compiler_params=_cparams(("parallel", "parallel", "arbitrary")),
        name="mlstm_group",
    )(z, z, z, z, gates_t, gates_t, conv_w, conv_w, conv_b, conv_b, norm_w.reshape(1, HEAD_DIM),
      conv_tail, conv_tail, c0, n0.reshape(bsz * h, 1, HEAD_DIM), m0.reshape(bsz * h, 1, 1))
    return (outs[0], outs[1], outs[2].reshape(bsz, h, HEAD_DIM), outs[3].reshape(bsz, h))


def _hgrn2_kernel(qz_ref, fz_ref, iz_ref, gz_ref, lb_ref, nw_ref, s0_ref, o_ref, sn_ref, s_s,
                  *, L, nc, t_valid):
    c = pl.program_id(2)

    @pl.when(c == 0)
    def _():
        s_s[...] = s0_ref[0, 0]

    lb = lb_ref[...]
    q = _silu(qz_ref[...])
    fg = lb + (1.0 - lb) * _sigmoid(fz_ref[...])
    k = 1.0 - fg
    lf = jnp.log(fg)
    v = iz_ref[...]
    if t_valid is not None:
        t_idx = c * L + lax.broadcasted_iota(jnp.int32, (L, HEAD_DIM), 0)
        lf = jnp.where(t_idx < t_valid, lf, 0.0)
        k = jnp.where(t_idx < t_valid, k, 0.0)

    ri = lax.broadcasted_iota(jnp.int32, (L, L), 0)
    ci = lax.broadcasted_iota(jnp.int32, (L, L), 1)
    tril = jnp.where(ci <= ri, 1.0, 0.0).astype(F32)
    b = jnp.dot(tril, lf, preferred_element_type=F32, precision=HIGHEST)
    s_prev = s_s[...]
    vb = v.astype(BF16)
    o_inter = jnp.dot((q * jnp.exp(b)).astype(BF16), s_prev.astype(BF16), preferred_element_type=F32)

    srow = lax.broadcasted_iota(jnp.int32, (L, HEAD_DIM), 0)
    trow = lax.broadcasted_iota(jnp.int32, (SUB, HEAD_DIM), 0)
    pieces = []
    for blk in range(L // SUB):
        r0 = blk * SUB
        q_i = q[r0:r0 + SUB]
        b_i = b[r0:r0 + SUB]
        o_i = o_inter[r0:r0 + SUB]
        if blk > 0:
            b_mid = b[r0 - 1:r0]
            qp = q_i * jnp.exp(b_i - b_mid)
            kp = k * jnp.exp(jnp.where(srow < r0, b_mid - b, NEG_INF))
            a = _dot_nt(qp.astype(BF16), kp.astype(BF16))
            o_i = o_i + jnp.dot(a.astype(BF16), vb, preferred_element_type=F32)
        for s in range(SUB):
            w = jnp.exp(jnp.where(trow >= s, b_i - b_i[s:s + 1], NEG_INF))
            a_col = jnp.sum(q_i * k[r0 + s:r0 + s + 1] * w, axis=1, keepdims=True)
            o_i = o_i + a_col * v[r0 + s:r0 + s + 1]
        pieces.append(o_i)
    o = jnp.concatenate(pieces, axis=0)

    b_last = b[L - 1:L]
    kw = k * jnp.exp(b_last - b)
    s_new = (_row_to_col(jnp.exp(b_last), HEAD_DIM) * s_prev
             + jnp.dot(kw.T.astype(BF16), vb, preferred_element_type=F32))
    s_s[...] = s_new

    o_ref[...] = (_rms(o, nw_ref[...]) * _silu(gz_ref[...])).astype(o_ref.dtype)

    @pl.when(c == nc - 1)
    def _():
        sn_ref[0, 0] = s_new


def hgrn2_group(z, col0, lb, norm_w, s0, *, bsz, t, n_heads, t_valid):
    L = 128
    nc = t // L
    h = n_heads
    zspec = lambda off: pl.BlockSpec((L, HEAD_DIM), lambda b, hh, c: (b * nc + c, col0 + off + hh))
    sspec = pl.BlockSpec((1, 1, HEAD_DIM, HEAD_DIM), lambda b, hh, c: (b, hh, 0, 0))
    outs = pl.pallas_call(
        functools.partial(_hgrn2_kernel, L=L, nc=nc, t_valid=t_valid),
        grid=(bsz, h, nc),
        in_specs=[zspec(0), zspec(h), zspec(2 * h), zspec(3 * h),
                  pl.BlockSpec((1, HEAD_DIM), lambda b, hh, c: (0, hh)),
                  pl.BlockSpec((1, HEAD_DIM), lambda b, hh, c: (0, 0)),
                  sspec],
        out_specs=[pl.BlockSpec((L, HEAD_DIM), lambda b, hh, c: (b * nc + c, hh)), sspec],
        out_shape=[jax.ShapeDtypeStruct((bsz * t, h * HEAD_DIM), BF16),
                   jax.ShapeDtypeStruct((bsz, h, HEAD_DIM, HEAD_DIM), F32)],
        scratch_shapes=[pltpu.VMEM((HEAD_DIM, HEAD_DIM), F32)],
        compiler_params=_cparams(("parallel", "parallel", "arbitrary")),
        name="hgrn2_group",
    )(z, z, z, z, lb, norm_w.reshape(1, HEAD_DIM), s0)
    return outs[0], outs[1]


def _fox_gate_kernel(g_ref, lf_ref, cum_ref, *, t):
    lf = _log_sigmoid(g_ref[0])
    lf_ref[0] = lf
    ri = lax.broadcasted_iota(jnp.int32, (128, 128), 0)
    ci = lax.broadcasted_iota(jnp.int32, (128, 128), 1)
    upper = jnp.where(ri <= ci, 1.0, 0.0).astype(F32)
    carry = jnp.zeros((lf.shape[0], 1), F32)
    for ch in range(t // 128):
        seg = lf[:, ch * 128:(ch + 1) * 128]
        cs = jnp.dot(seg, upper, preferred_element_type=F32, precision=HIGHEST) + carry
        cum_ref[0, :, ch * 128:(ch + 1) * 128] = cs
        carry = cs[:, 127:128]


def fox_gates(gates_t, row_block, n_heads):
    bsz, _, t = gates_t.shape
    spec = pl.BlockSpec((1, n_heads, t), lambda b: (b, 0, 0))
    return pl.pallas_call(
        functools.partial(_fox_gate_kernel, t=t),
        grid=(bsz,),
        in_specs=[pl.BlockSpec((1, n_heads, t), lambda b: (b, row_block, 0))],
        out_specs=[spec, spec],
        out_shape=[jax.ShapeDtypeStruct((bsz, n_heads, t), F32)] * 2,
        compiler_params=_cparams(("parallel",)),
        name="fox_gates",
    )(gates_t)


def _fox_prefill_kernel(q_ref, k_ref, v_ref, fq_ref, fk_ref, o_ref, *, tq, tk):
    qi = pl.program_id(2)
    q = q_ref[...].astype(BF16)
    fq_col = _row_to_col(fq_ref[0], tq)
    qpos = qi * tq + lax.broadcasted_iota(jnp.int32, (tq, tk), 0)
    kloc = lax.broadcasted_iota(jnp.int32, (tq, tk), 1)
    scale = HEAD_DIM ** -0.5

    def step(ki, carry):
        m, l, acc = carry
        start = pl.multiple_of(ki * tk, tk)
        kb = k_ref[pl.ds(start, tk), :].astype(BF16)
        vb = v_ref[pl.ds(start, tk), :].astype(BF16)
        s = _dot_nt(q, kb) * scale + fq_col - fk_ref[0, ki]
        s = jnp.where(ki * tk + kloc <= qpos, s, NEG_INF)
        m_new = jnp.maximum(m, jnp.max(s, axis=1, keepdims=True))
        corr = jnp.exp(m - m_new)
        p = jnp.exp(s - m_new)
        l = l * corr + jnp.sum(p, axis=1, keepdims=True)
        acc = acc * corr + jnp.dot(p.astype(BF16), vb, preferred_element_type=F32)
        return m_new, l, acc

    init = (jnp.full((tq, 1), NEG_INF, F32), jnp.zeros((tq, 1), F32), jnp.zeros((tq, HEAD_DIM), F32))
    n_blocks = (qi * tq + tq + tk - 1) // tk
    _, l, acc = lax.fori_loop(0, n_blocks, step, init)
    o_ref[...] = (acc / l).astype(o_ref.dtype)


def fox_prefill(z, col0, f_cum, *, bsz, t, n_heads):
    tq = min(256, t)
    tk = min(512, t)
    h = n_heads
    nq = t // tq
    fq = f_cum.reshape(bsz * h, 1, t)
    fk = f_cum.reshape(bsz * h, t // tk, 1, tk)
    return pl.pallas_call(
        functools.partial(_fox_prefill_kernel, tq=tq, tk=tk),
        grid=(bsz, h, nq),
        in_specs=[pl.BlockSpec((tq, HEAD_DIM), lambda b, hh, i: (b * nq + i, col0 + hh)),
                  pl.BlockSpec((t, HEAD_DIM), lambda b, hh, i: (b, col0 + h + hh)),
                  pl.BlockSpec((t, HEAD_DIM), lambda b, hh, i: (b, col0 + 2 * h + hh)),
                  pl.BlockSpec((1, 1, tq), lambda b, hh, i: (b * h + hh, 0, i)),
                  pl.BlockSpec((1, t // tk, 1, tk), lambda b, hh, i: (b * h + hh, 0, 0, 0))],
        out_specs=pl.BlockSpec((tq, HEAD_DIM), lambda b, hh, i: (b * nq + i, hh)),
        out_shape=jax.ShapeDtypeStruct((bsz * t, h * HEAD_DIM), BF16),
        compiler_params=_cparams(("parallel", "parallel", "arbitrary")),
        name="fox_prefill",
    )(z, z, z, fq, fk)


def _fox_decode_kernel(pt_ref, qt_ref, kp_ref, vp_ref, lfe_ref, grow_ref, knew_ref, vnew_ref, bnew_ref,
                       o_ref, m_s, l_s, acc_s, carry_s, *, n_pages, n_heads, page, dec_seq):
    j = pl.program_id(1)
    rows = page * n_heads
    cols = n_heads * 8

    @pl.when(j == 0)
    def _():
        m_s[...] = jnp.full_like(m_s, NEG_INF)
        l_s[...] = jnp.zeros_like(l_s)
        acc_s[...] = jnp.zeros_like(acc_s)
        carry_s[...] = jnp.zeros_like(carry_s)

    qt = qt_ref[0]
    r_i = lax.broadcasted_iota(jnp.int32, (rows, cols), 0)
    c_i = lax.broadcasted_iota(jnp.int32, (rows, cols), 1)
    same_head = lax.rem(r_i, n_heads) == lax.div(c_i, 8)
    src_of_row = lax.div(lax.broadcasted_iota(jnp.int32, (rows, page), 0), n_heads)
    sel = jnp.where(src_of_row == lax.broadcasted_iota(jnp.int32, (rows, page), 1), 1.0, 0.0).astype(BF16)

    def update(k3_ref, v3_ref, bias, mask):
        k2 = k3_ref[...].reshape(rows, HEAD_DIM).astype(BF16)
        v2 = v3_ref[...].reshape(rows, HEAD_DIM).astype(BF16)
        hi = bias.astype(BF16)
        rem = bias - hi.astype(F32)
        mid = rem.astype(BF16)
        lo = (rem - mid.astype(F32)).astype(BF16)
        s = (jnp.dot(k2, qt, preferred_element_type=F32) + jnp.dot(sel, hi, preferred_element_type=F32)
             + jnp.dot(sel, mid, preferred_element_type=F32) + jnp.dot(sel, lo, preferred_element_type=F32))
        s = jnp.where(mask, s, NEG_INF)
        m_old = m_s[...]
        m_new = jnp.maximum(m_old, jnp.max(s, axis=0, keepdims=True))
        corr = jnp.exp(m_old - m_new)
        p = jnp.exp(s - m_new)
        l_s[...] = l_s[...] * corr + jnp.sum(p, axis=0, keepdims=True)
        pv = jnp.dot(p.T.astype(BF16), v2, preferred_element_type=F32)
        acc_s[...] = acc_s[...] * _row_to_col(corr, cols) + pv
        m_s[...] = m_new

    lfe = lfe_ref[...]
    ri = lax.broadcasted_iota(jnp.int32, (page, page), 0)
    ci = lax.broadcasted_iota(jnp.int32, (page, page), 1)
    later = jnp.where(ci > ri, 1.0, 0.0).astype(F32)
    suffix = jnp.dot(later, lfe, preferred_element_type=F32, precision=HIGHEST) + carry_s[...]
    update(kp_ref, vp_ref, suffix + grow_ref[0], same_head)
    carry_s[...] = carry_s[...] + jnp.sum(lfe, axis=0, keepdims=True)

    @pl.when(j == n_pages - 1)
    def _():
        causal = jnp.logical_and(lax.div(r_i, n_heads) <= lax.rem(c_i, 8), lax.div(r_i, n_heads) < dec_seq)
        update(knew_ref.at[0], vnew_ref.at[0], bnew_ref[0], jnp.logical_and(same_head, causal))
        o_ref[0] = (acc_s[...] / _row_to_col(l_s[...], cols)).astype(o_ref.dtype)


def fox_decode(qt, k_pool, v_pool, lf_exp, layer, page_table, g_row, k_new, v_new, b_new, *, n_heads, dec_seq):
    bsz, n_pages = page_table.shape
    page = k_pool.shape[2]
    cols = n_heads * 8
    pool_map = lambda b, j, pt: (layer, pt[b, n_pages - 1 - j], 0, 0, 0)
    per_b3 = lambda b, j, pt: (b, 0, 0)
    per_b4 = lambda b, j, pt: (b, 0, 0, 0)
    grid_spec = pltpu.PrefetchScalarGridSpec(
        num_scalar_prefetch=1,
        grid=(bsz, n_pages),
        in_specs=[pl.BlockSpec((1, HEAD_DIM, cols), per_b3),
                  pl.BlockSpec((None, None, page, n_heads, HEAD_DIM), pool_map),
                  pl.BlockSpec((None, None, page, n_heads, HEAD_DIM), pool_map),
                  pl.BlockSpec((None, None, page, cols), lambda b, j, pt: (layer, pt[b, n_pages - 1 - j], 0, 0)),
                  pl.BlockSpec((1, 1, cols), per_b3),
                  pl.BlockSpec((1, page, n_heads, HEAD_DIM), per_b4),
                  pl.BlockSpec((1, page, n_heads, HEAD_DIM), per_b4),
                  pl.BlockSpec((1, page, cols), per_b3)],
        out_specs=pl.BlockSpec((1, cols, HEAD_DIM), per_b3),
        scratch_shapes=[pltpu.VMEM((1, cols), F32), pltpu.VMEM((1, cols), F32),
                        pltpu.VMEM((cols, HEAD_DIM), F32), pltpu.VMEM((1, cols), F32)],
    )
    return pl.pallas_call(
        functools.partial(_fox_decode_kernel, n_pages=n_pages, n_heads=n_heads, page=page, dec_seq=dec_seq),
        grid_spec=grid_spec,
        out_shape=jax.ShapeDtypeStruct((bsz, cols, HEAD_DIM), BF16),
        compiler_params=_cparams(("parallel", "arbitrary")),
        name="fox_decode",
    )(page_table, qt, k_pool, v_pool, lf_exp, g_row, k_new, v_new, b_new)


def _cross_attn_kernel(q_ref, k_ref, v_ref, o_ref):
    s = _dot_nt(q_ref[0].astype(BF16), k_ref[0].astype(BF16)) * (HEAD_DIM ** -0.5)
    m = jnp.max(s, axis=1, keepdims=True)
    p = jnp.exp(s - m)
    o = jnp.dot(p.astype(BF16), v_ref[0].astype(BF16), preferred_element_type=F32)
    o_ref[0] = (o / jnp.sum(p, axis=1, keepdims=True)).astype(o_ref.dtype)


def cross_attn(q, mem_k, mem_v):
    bsz, t, w = q.shape
    n_mem = mem_k.shape[1]
    h = w // HEAD_DIM
    tq = _pick_tile(t, 512, 8)
    return pl.pallas_call(
        _cross_attn_kernel,
        grid=(bsz, h, t // tq),
        in_specs=[pl.BlockSpec((1, tq, HEAD_DIM), lambda b, hh, i: (b, i, hh)),
                  pl.BlockSpec((1, n_mem, HEAD_DIM), lambda b, hh, i: (b, 0, hh)),
                  pl.BlockSpec((1, n_mem, HEAD_DIM), lambda b, hh, i: (b, 0, hh))],
        out_specs=pl.BlockSpec((1, tq, HEAD_DIM), lambda b, hh, i: (b, i, hh)),
        out_shape=jax.ShapeDtypeStruct((bsz, t, w), BF16),
        compiler_params=_cparams(("parallel", "parallel", "parallel")),
        name="cross_attn",
    )(q, mem_k, mem_v)


def _router_kernel(x_ref, g_ref, wr_ref, br_ref, o_ref, *, n_experts):
    xn = _rms(x_ref[...], g_ref[...])
    logits = lax.dot_general(wr_ref[...], xn, (((1,), (1,)), ((), ())),
                             preferred_element_type=F32, precision=HIGHEST) + br_ref[...]
    e_idx = lax.broadcasted_iota(jnp.int32, logits.shape, 0)
    v1 = jnp.max(logits, axis=0, keepdims=True)
    i1 = jnp.min(jnp.where(logits == v1, e_idx, n_experts), axis=0, keepdims=True)
    rest = jnp.where(e_idx == i1, NEG_INF, logits)
    v2 = jnp.max(rest, axis=0, keepdims=True)
    i2 = jnp.min(jnp.where(rest == v2, e_idx, n_experts), axis=0, keepdims=True)
    e2 = jnp.exp(v2 - v1)
    g1 = 1.0 / (1.0 + e2)
    g2 = e2 / (1.0 + e2)
    zeros = jnp.zeros((4, logits.shape[1]), F32)
    o_ref[...] = jnp.concatenate([i1.astype(F32), i2.astype(F32), g1, g2, zeros], axis=0)


def route_top2(x, g_pre, w_router, b_router):
    r, d = x.shape
    n_experts = w_router.shape[1]
    tr = _pick_tile(r, 256, 128)
    return pl.pallas_call(
        functools.partial(_router_kernel, n_experts=n_experts),
        grid=(pl.cdiv(r, tr),),
        in_specs=[pl.BlockSpec((tr, d), lambda i: (i, 0)),
                  pl.BlockSpec((1, d), lambda i: (0, 0)),
                  pl.BlockSpec((n_experts, d), lambda i: (0, 0)),
                  pl.BlockSpec((n_experts, 1), lambda i: (0, 0))],
        out_specs=pl.BlockSpec((8, tr), lambda i: (0, i)),
        out_shape=jax.ShapeDtypeStruct((8, r), F32),
        compiler_params=_cparams(("parallel",)),
        name="route_top2",
    )(x, g_pre.reshape(1, d).astype(F32), w_router.T.astype(F32), b_router.reshape(n_experts, 1).astype(F32))


def _dispatch_kernel(src_ref, x_hbm, g_ref, o_ref, buf, sem, *, tg):
    base = pl.program_id(0) * tg

    def row_copy(j):
        return pltpu.make_async_copy(x_hbm.at[pl.ds(src_ref[base + j], 1)], buf.at[pl.ds(j, 1)], sem)

    def start(j, carry):
        row_copy(j).start()
        return carry

    def wait(j, carry):
        row_copy(j).wait()
        return carry

    lax.fori_loop(0, tg, start, 0)
    lax.fori_loop(0, tg, wait, 0)
    o_ref[...] = _rms(buf[...], g_ref[...]).astype(o_ref.dtype)


def dispatch_rows(x, g_pre, src_rows):
    r, d = x.shape
    n_out = src_rows.shape[0]
    tg = 256
    assert n_out % tg == 0
    grid_spec = pltpu.PrefetchScalarGridSpec(
        num_scalar_prefetch=1,
        grid=(n_out // tg,),
        in_specs=[pl.BlockSpec(memory_space=pl.ANY), pl.BlockSpec((1, d), lambda i, src: (0, 0))],
        out_specs=pl.BlockSpec((tg, d), lambda i, src: (i, 0)),
        scratch_shapes=[pltpu.VMEM((tg, d), F32), pltpu.SemaphoreType.DMA(())],
    )
    return pl.pallas_call(
        functools.partial(_dispatch_kernel, tg=tg),
        grid_spec=grid_spec,
        out_shape=jax.ShapeDtypeStruct((n_out, d), BF16),
        compiler_params=_cparams(("arbitrary",)),
        name="dispatch_rows",
    )(src_rows, x, g_pre.reshape(1, d).astype(F32))


def _combine_kernel(p1_ref, p2_ref, y_hbm, g1_ref, g2_ref, o_ref, buf1, buf2, sem, *, tc):
    base = pl.program_id(0) * tc

    def copies(j):
        return (pltpu.make_async_copy(y_hbm.at[pl.ds(p1_ref[base + j], 1)], buf1.at[pl.ds(j, 1)], sem.at[0]),
                pltpu.make_async_copy(y_hbm.at[pl.ds(p2_ref[base + j], 1)], buf2.at[pl.ds(j, 1)], sem.at[1]))

    def start(j, carry):
        for cp in copies(j):
            cp.start()
        return carry

    def wait(j, carry):
        for cp in copies(j):
            cp.wait()
        return carry

    lax.fori_loop(0, tc, start, 0)
    lax.fori_loop(0, tc, wait, 0)
    g1 = _row_to_col(g1_ref[0], tc)
    g2 = _row_to_col(g2_ref[0], tc)
    o_ref[...] = g1 * buf1[...] + g2 * buf2[...]


def combine_rows(y, pos1, pos2, gate1, gate2, n_rows):
    d = y.shape[1]
    tc = 128
    n_pad = pos1.shape[0]
    assert n_pad % tc == 0 and n_pad >= n_rows
    grid_spec = pltpu.PrefetchScalarGridSpec(
        num_scalar_prefetch=2,
        grid=(n_pad // tc,),
        in_specs=[pl.BlockSpec(memory_space=pl.ANY),
                  pl.BlockSpec((1, 1, tc), lambda i, p1, p2: (i, 0, 0)),
                  pl.BlockSpec((1, 1, tc), lambda i, p1, p2: (i, 0, 0))],
        out_specs=pl.BlockSpec((tc, d), lambda i, p1, p2: (i, 0)),
        scratch_shapes=[pltpu.VMEM((tc, d), F32), pltpu.VMEM((tc, d), F32), pltpu.SemaphoreType.DMA((2,))],
    )
    return pl.pallas_call(
        functools.partial(_combine_kernel, tc=tc),
        grid_spec=grid_spec,
        out_shape=jax.ShapeDtypeStruct((n_rows, d), F32),
        compiler_params=_cparams(("arbitrary",)),
        name="combine_rows",
    )(pos1, pos2, y, gate1.reshape(n_pad // tc, 1, tc), gate2.reshape(n_pad // tc, 1, tc))


MOE_TM = 512
MOE_ROW_SPLITS = 2


def moe_ffn(x, g_pre, w_router, b_router, w1, w3, w2):
    r, d = x.shape
    n_experts = w1.shape[0]
    d_ff = w1.shape[2]
    tm = MOE_TM
    routed = route_top2(x, g_pre, w_router, b_router)
    idx = routed[0:2].astype(jnp.int32)
    gates = routed[2:4]

    e_flat = idx.reshape(-1)
    onehot = (e_flat[:, None] == jnp.arange(n_experts, dtype=jnp.int32)[None, :]).astype(jnp.int32)
    rank = jnp.sum((jnp.cumsum(onehot, axis=0) - onehot) * onehot, axis=1)
    counts = jnp.sum(onehot, axis=0)
    tiles = (counts + tm - 1) // tm
    tile_end = jnp.cumsum(tiles)
    tile_start = tile_end - tiles
    pos = (tile_start * tm)[e_flat] + rank
    n_tiles = _round_up((TOP_K * r + n_experts * (tm - 1)) // tm + 1, MOE_ROW_SPLITS)
    token = jnp.tile(jnp.arange(r, dtype=jnp.int32), TOP_K)
    src_rows = jnp.zeros((n_tiles * tm,), jnp.int32).at[pos].set(token)
    t_idx = jnp.arange(n_tiles, dtype=jnp.int32)
    used = tile_end[-1]
    t_clamped = jnp.minimum(t_idx, used - 1)
    tile_expert = jnp.sum((t_clamped[:, None] >= tile_end[None, :]).astype(jnp.int32), axis=1)
    tile_valid = (t_idx < used).astype(jnp.int32)
    tile_first = jnp.logical_and(tile_valid == 1,
                                 jnp.any(t_idx[:, None] == tile_start[None, :], axis=1)).astype(jnp.int32)
    tile_first = tile_first.at[0].set(1)
    groups = (tile_expert.astype(jnp.int32), tile_first, tile_valid)

    xs = dispatch_rows(x, g_pre, src_rows)
    hid = matmul(xs, [w1, w3], out_dtype=BF16, tm=tm, tn=512, groups=groups, name="moe_up")
    y = matmul(hid, [w2], tm=tm, tn=512, tk=_div_tile(d_ff, 3584), groups=groups,
               row_splits=MOE_ROW_SPLITS, name="moe_down")

    tc = 128
    n_pad = _round_up(r, tc)
    padi = lambda a: jnp.pad(a, (0, n_pad - r))
    return combine_rows(y, padi(pos[:r]), padi(pos[r:]), padi(gates[0]), padi(gates[1]), r)


def kernel(x_prompt, x_sample, cache_fox_k, cache_fox_v, cache_fox_logf, cache_mem_k, cache_mem_v,
           state_ml_conv, state_ml_C, state_ml_n, state_ml_m, state_hg_S, page_table, mem_prompt,
           w_in, b_in, ml_conv_w, ml_conv_b, ml_norm_w, hg_lower_bounds, hg_norm_w, w_out,
           ca_mem_norm, ca_w_q, ca_w_k, ca_w_v, ca_w_o,
           norm_mix_pre, norm_mix_post, norm_ca_pre, norm_ca_post, norm_ffn_pre, norm_ffn_post,
           ffn_w1, ffn_w3, ffn_w2, moe_router, moe_router_b, moe_w1, moe_w3, moe_w2):
    bp, seq, d_model = x_prompt.shape
    bs, dec_seq, _ = x_sample.shape
    depth = w_in.shape[0]
    ml_h = state_ml_C.shape[2]
    hg_h = state_hg_S.shape[2]
    fx_h = cache_fox_k.shape[3]
    ml_w, hg_w, fx_w = ml_h * HEAD_DIM, hg_h * HEAD_DIM, fx_h * HEAD_DIM
    n_pool, page = cache_fox_k.shape[1], cache_fox_k.shape[2]
    n_mem = mem_prompt.shape[1]
    ca_w = ca_w_q.shape[2]
    ca_h = ca_w // HEAD_DIM
    n_p, n_s = bp * seq, bs * dec_seq
    assert dec_seq <= 8 and seq % 256 == 0 or seq == 128

    sizes = (2 * ml_w, ml_w, ml_w, ml_h, ml_h, hg_w, hg_w, hg_w, hg_w, fx_w, fx_w, fx_w, fx_h)
    starts = [0]
    for s in sizes:
        starts.append(starts[-1] + s)
    wide = [(0, starts[3]), (starts[5], starts[12])]
    narrow = [(starts[3], starts[5]), (starts[12], starts[13])]
    n_gate = sum(b - a for a, b in narrow)
    fx_col = 4 * hg_h

    lb_soft = jax.nn.softmax(hg_lower_bounds.astype(F32), axis=0)
    lower_bounds = jnp.cumsum(lb_soft, axis=0) - lb_soft[0]
    lf_exp = jnp.repeat(cache_fox_logf.astype(F32), 8, axis=-1)

    x = jnp.concatenate([x_prompt.reshape(n_p, d_model), x_sample.reshape(n_s, d_model)], axis=0)
    h = rmsnorm_rows(x, norm_mix_pre[0], BF16)

    l_ml = 128
    acc = {name: [] for name in ('p_k', 'p_v', 'p_lf', 'p_mk', 'p_mv', 'p_conv', 'p_C', 'p_n', 'p_m', 'p_S',
                                 's_k', 's_v', 's_lf', 's_conv', 's_C', 's_n', 's_m', 's_S')}
    gate_cols = lambda a: jnp.concatenate([a[..., s:e] for s, e in narrow], axis=-1)
    for l in range(depth):
        za = matmul(h, [w_in], b_in[l, wide[0][0]:wide[0][1]], tm=1024, tn=512, lead=l,
                    col0=wide[0][0], n_cols=wide[0][1] - wide[0][0], name="in_proj_ml")
        zb = matmul(h, [w_in], b_in[l, wide[1][0]:wide[1][1]], tm=1024, tn=512, lead=l,
                    col0=wide[1][0], n_cols=wide[1][1] - wide[1][0], name="in_proj_hgfx")
        zg = matmul(h, [gate_cols(w_in[l])], gate_cols(b_in[l]), tm=1024, tn=128, name="in_proj_gates")
        pad_s = lambda a: jnp.pad(a[n_p:].reshape(bs, dec_seq, -1),
                                  ((0, 0), (0, l_ml - dec_seq), (0, 0)))
        za_s = pad_s(za).reshape(bs * l_ml, -1)
        zb_s = pad_s(zb).reshape(bs * l_ml, -1)
        g_p = zg[:n_p].reshape(bp, seq, n_gate).transpose(0, 2, 1)
        g_s = pad_s(zg).transpose(0, 2, 1)

        qk_p = za[:n_p, :2 * ml_w].reshape(bp, seq, 2 * ml_w)
        qk_s = za[n_p:, :2 * ml_w].reshape(bs, dec_seq, 2 * ml_w)
        conv_p = qk_p[:, seq - (CONV_W - 1):]
        conv_s = jnp.concatenate([state_ml_conv[l], qk_s], axis=1)[:, dec_seq:]
        tail_p = jnp.zeros((bp, 8, 2 * ml_w), F32)
        tail_s = jnp.pad(state_ml_conv[l], ((0, 0), (8 - (CONV_W - 1), 0), (0, 0)))

        lb_l = lower_bounds[l].reshape(1, hg_w)
        hml_p, c_p, nrm_p, m_p = mlstm_group(
            za, g_p.reshape(bp * n_gate, 1, seq), ml_conv_w[l], ml_conv_b[l].reshape(1, -1), ml_norm_w[l],
            tail_p, jnp.zeros((bp, ml_h, HEAD_DIM, HEAD_DIM), F32), jnp.zeros((bp, ml_h, HEAD_DIM), F32),
            jnp.zeros((bp, ml_h), F32), bsz=bp, t=seq, n_heads=ml_h, t_valid=None)
        hml_s, c_s, nrm_s, m_s = mlstm_group(
            za_s, g_s.reshape(bs * n_gate, 1, l_ml), ml_conv_w[l], ml_conv_b[l].reshape(1, -1), ml_norm_w[l],
            tail_s, state_ml_C[l], state_ml_n[l], state_ml_m[l], bsz=bs, t=l_ml, n_heads=ml_h, t_valid=dec_seq)
        ohg_p, s_p = hgrn2_group(zb, 0, lb_l, hg_norm_w[l], jnp.zeros((bp, hg_h, HEAD_DIM, HEAD_DIM), F32),
                                 bsz=bp, t=seq, n_heads=hg_h, t_valid=None)
        ohg_s, s_s = hgrn2_group(zb_s, 0, lb_l, hg_norm_w[l], state_hg_S[l],
                                 bsz=bs, t=l_ml, n_heads=hg_h, t_valid=dec_seq)
        fx_row_block = (2 * ml_h) // fx_h
        assert fx_row_block * fx_h == 2 * ml_h
        lf_p, fcum_p = fox_gates(g_p, fx_row_block, fx_h)
        lf_s, fcum_s = fox_gates(g_s, fx_row_block, fx_h)
        ofx_p = fox_prefill(zb, fx_col, fcum_p, bsz=bp, t=seq, n_heads=fx_h)

        fx0 = fx_col * HEAD_DIM
        q_s = zb[n_p:, fx0:fx0 + fx_w].reshape(bs, dec_seq, fx_h, HEAD_DIM)
        k_s = zb[n_p:, fx0 + fx_w:fx0 + 2 * fx_w].reshape(bs, dec_seq, fx_w)
        v_s = zb[n_p:, fx0 + 2 * fx_w:fx0 + 3 * fx_w].reshape(bs, dec_seq, fx_w)
        qt = jnp.pad(q_s * (HEAD_DIM ** -0.5), ((0, 0), (0, 8 - dec_seq), (0, 0), (0, 0)))
        qt = qt.transpose(0, 3, 2, 1).reshape(bs, HEAD_DIM, fx_h * 8).astype(BF16)
        g_tok = fcum_s[:, :, :8]
        g_row = jnp.where(jnp.arange(8)[None, None, :] < dec_seq, g_tok, 0.0).reshape(bs, 1, fx_h * 8)
        g_src = fcum_s[:, :, :page].transpose(0, 2, 1)
        b_new = (g_tok[:, None, :, :] - g_src[:, :, :, None]).reshape(bs, page, fx_h * 8)
        padk = lambda a: jnp.pad(a, ((0, 0), (0, page - dec_seq), (0, 0))).reshape(bs, page, fx_h, HEAD_DIM)
        ofx_s = fox_decode(qt, cache_fox_k, cache_fox_v, lf_exp, l, page_table, g_row, padk(k_s), padk(v_s),
                           b_new, n_heads=fx_h, dec_seq=dec_seq)
        ofx_s = ofx_s.reshape(bs, fx_h, 8, HEAD_DIM)[:, :, :dec_seq].transpose(0, 2, 1, 3).reshape(n_s, fx_w)

        unpad = lambda a: a.reshape(bs, l_ml, -1)[:, :dec_seq].reshape(n_s, -1)
        merged = jnp.concatenate([
            jnp.concatenate([hml_p, ohg_p, ofx_p], axis=1),
            jnp.concatenate([unpad(hml_s), unpad(ohg_s), ofx_s], axis=1)], axis=0)
        mix = matmul(merged, [w_out], tm=1024, tn=512, lead=l, name="out_proj")
        x, h_ca = add_norm(x, mix, norm_mix_post[l], norm_ca_pre[l])

        hm = rmsnorm_rows(mem_prompt.reshape(bp * n_mem, d_model), ca_mem_norm[l], BF16)
        mk = matmul(hm, [ca_w_k], tm=1024, tn=512, lead=l, name="mem_k")
        mv = matmul(hm, [ca_w_v], tm=1024, tn=512, lead=l, name="mem_v")
        qc = matmul(h_ca, [ca_w_q], tm=1024, tn=512, lead=l, name="ca_q")
        o_p = cross_attn(qc[:n_p].reshape(bp, seq, ca_w), mk.reshape(bp, n_mem, ca_w), mv.reshape(bp, n_mem, ca_w))
        qc_s = jnp.pad(qc[n_p:].reshape(bs, dec_seq, ca_w), ((0, 0), (0, 8 - dec_seq), (0, 0)))
        o_s = cross_attn(qc_s, cache_mem_k[l].reshape(bs, n_mem, ca_w), cache_mem_v[l].reshape(bs, n_mem, ca_w))
        o_ca = jnp.concatenate([o_p.reshape(n_p, ca_w), o_s[:, :dec_seq].reshape(n_s, ca_w)], axis=0)
        c_out = matmul(o_ca, [ca_w_o], tm=1024, tn=1024, lead=l, name="ca_o")

        j = l // 2
        g_next = norm_mix_pre[l + 1] if l + 1 < depth else None
        if l % 2 == 0:
            x, h_ffn = add_norm(x, c_out, norm_ca_post[l], norm_ffn_pre[l])
            d_ff = ffn_w1.shape[2]
            hid = matmul(h_ffn, [ffn_w1, ffn_w3], out_dtype=BF16, tm=512, tn=512, lead=j, name="ffn_up")
            f_out = matmul(hid, [ffn_w2], tm=1024, tn=512, tk=_div_tile(d_ff, 2048), lead=j, name="ffn_down")
        else:
            x, _ = add_norm(x, c_out, norm_ca_post[l])
            f_out = moe_ffn(x, norm_ffn_pre[l], moe_router[j], moe_router_b[j], moe_w1[j], moe_w3[j], moe_w2[j])
        x, h = add_norm(x, f_out, norm_ffn_post[l], g_next)

        k0, v0 = fx0 + fx_w, fx0 + 2 * fx_w
        acc['p_k'].append(zb[:n_p, k0:k0 + fx_w].reshape(bp, seq, fx_h, HEAD_DIM))
        acc['p_v'].append(zb[:n_p, v0:v0 + fx_w].reshape(bp, seq, fx_h, HEAD_DIM))
        acc['p_lf'].append(lf_p.transpose(0, 2, 1))
        acc['p_mk'].append(mk.reshape(bp, n_mem, ca_h, HEAD_DIM))
        acc['p_mv'].append(mv.reshape(bp, n_mem, ca_h, HEAD_DIM))
        acc['p_conv'].append(conv_p)
        acc['p_C'].append(c_p)
        acc['p_n'].append(nrm_p)
        acc['p_m'].append(m_p)
        acc['p_S'].append(s_p)
        acc['s_k'].append(k_s.reshape(bs, dec_seq, fx_h, HEAD_DIM))
        acc['s_v'].append(v_s.reshape(bs, dec_seq, fx_h, HEAD_DIM))
        acc['s_lf'].append(lf_s[:, :, :dec_seq].transpose(0, 2, 1))
        acc['s_conv'].append(conv_s)
        acc['s_C'].append(c_s)
        acc['s_n'].append(nrm_s)
        acc['s_m'].append(m_s)
        acc['s_S'].append(s_s)

    y_p = x[:n_p].reshape(bp, seq, d_model)
    y_s = x[n_p:].reshape(bs, dec_seq, d_model)
    order = ('p_k', 'p_v', 'p_lf', 'p_mk', 'p_mv', 'p_conv', 'p_C', 'p_n', 'p_m', 'p_S',
             's_k', 's_v', 's_lf', 's_conv', 's_C', 's_n', 's_m', 's_S')
    return (y_p, y_s) + tuple(jnp.stack(acc[name]) for name in order)
```

```python
import functools

import jax
import jax.numpy as jnp
from jax import lax
from jax.experimental import pallas as pl
from jax.experimental.pallas import tpu as pltpu

F32 = jnp.float32
BF16 = jnp.bfloat16
EPS = 1e-6
HEAD_DIM = 128
CONV_W = 4
TOP_K = 2
SUB = 16
VMEM_LIMIT_BYTES = 56 * 1024 * 1024
HIGHEST = lax.Precision.HIGHEST
NEG_INF = float("-inf")


def _cparams(sem):
    return pltpu.CompilerParams(dimension_semantics=sem, vmem_limit_bytes=VMEM_LIMIT_BYTES)


def _round_up(a, b):
    return (a + b - 1) // b * b


def _pick_tile(n, target, quantum):
    if n <= quantum:
        return n
    return max(quantum, min(target, n) // quantum * quantum)


def _div_tile(n, target):
    if n <= target:
        return n
    t = target // 128 * 128
    while t > 128 and n % t:
        t -= 128
    assert n % t == 0, (n, target)
    return t


def _log_sigmoid(x):
    return jnp.minimum(x, 0.0) - jnp.log(1.0 + jnp.exp(-jnp.abs(x)))


def _sigmoid(x):
    return 1.0 / (1.0 + jnp.exp(-x))


def _silu(x):
    return x * _sigmoid(x)


def _dot_nt(a, b):
    return lax.dot_general(a, b, (((1,), (1,)), ((), ())), preferred_element_type=F32)


def _row_to_col(row, n):
    r = lax.broadcasted_iota(jnp.int32, (n, n), 0)
    c = lax.broadcasted_iota(jnp.int32, (n, n), 1)
    return jnp.sum(jnp.where(r == c, row, 0.0), axis=1, keepdims=True)


def _col_to_row(col, n):
    r = lax.broadcasted_iota(jnp.int32, (n, n), 0)
    c = lax.broadcasted_iota(jnp.int32, (n, n), 1)
    return jnp.sum(jnp.where(r == c, col, 0.0), axis=0, keepdims=True)


def _rms(x, g):
    return x * lax.rsqrt(jnp.mean(x * x, axis=-1, keepdims=True) + EPS) * g


def _rmsnorm_kernel(x_ref, g_ref, o_ref):
    o_ref[...] = _rms(x_ref[...].astype(F32), g_ref[...]).astype(o_ref.dtype)


def rmsnorm_rows(x, g, out_dtype):
    r, d = x.shape
    tr = _pick_tile(r, 256, 8)
    return pl.pallas_call(
        _rmsnorm_kernel,
        grid=(pl.cdiv(r, tr),),
        in_specs=[pl.BlockSpec((tr, d), lambda i: (i, 0)), pl.BlockSpec((1, d), lambda i: (0, 0))],
        out_specs=pl.BlockSpec((tr, d), lambda i: (i, 0)),
        out_shape=jax.ShapeDtypeStruct((r, d), out_dtype),
        compiler_params=_cparams(("parallel",)),
        name="rmsnorm_rows",
    )(x, g.reshape(1, d).astype(F32))


def _add_norm_kernel(x_ref, y_ref, gpost_ref, *rest, with_next):
    x_new = x_ref[...] + _rms(y_ref[...], gpost_ref[...])
    if with_next:
        gpre_ref, xo_ref, ho_ref = rest
        ho_ref[...] = _rms(x_new, gpre_ref[...]).astype(ho_ref.dtype)
    else:
        (xo_ref,) = rest
    xo_ref[...] = x_new


def add_norm(x, y, g_post, g_pre=None):
    r, d = x.shape
    tr = _pick_tile(r, 256, 8)
    row = pl.BlockSpec((tr, d), lambda i: (i, 0))
    vec = pl.BlockSpec((1, d), lambda i: (0, 0))
    with_next = g_pre is not None
    ins = [x, y, g_post.reshape(1, d).astype(F32)]
    in_specs = [row, row, vec]
    out_shape = [jax.ShapeDtypeStruct((r, d), F32)]
    out_specs = [row]
    if with_next:
        ins.append(g_pre.reshape(1, d).astype(F32))
        in_specs.append(vec)
        out_shape.append(jax.ShapeDtypeStruct((r, d), BF16))
        out_specs.append(row)
    outs = pl.pallas_call(
        functools.partial(_add_norm_kernel, with_next=with_next),
        grid=(pl.cdiv(r, tr),),
        in_specs=in_specs, out_specs=out_specs, out_shape=out_shape,
        compiler_params=_cparams(("parallel",)),
        name="add_norm",
    )(*ins)
    return (outs[0], outs[1]) if with_next else (outs[0], None)


def _mm_kernel(*refs, n_w, has_bias, nk, nm, tm, tail, cast_w, grouped, shift, tn):
    if grouped:
        te_ref, tf_ref, tv_ref = refs[:3]
        refs = refs[3:]
    x_ref = refs[0]
    pos = 1
    w_refs = []
    for _ in range(n_w):
        if shift:
            w_refs.append((refs[pos], refs[pos + 1]))
            pos += 2
        else:
            w_refs.append((refs[pos],))
            pos += 1
    b_ref = None
    if has_bias:
        b_ref = refs[pos]
        pos += 1
    o_ref = refs[pos]
    pos += 1
    if cast_w:
        wb_refs = refs[pos:pos + n_w]
        pos += n_w
    else:
        wb_refs = [w[0] for w in w_refs]
    acc_ref = refs[pos] if nk > 1 else None

    k = pl.program_id(2)
    ml = pl.program_id(3)
    m = pl.program_id(0) * nm + ml

    if cast_w:
        first = jnp.logical_or(tf_ref[m] == 1, ml == 0) if grouped else (ml == 0)

        @pl.when(first)
        def _():
            for w, wb_ref in zip(w_refs, wb_refs):
                if shift:
                    wide = jnp.concatenate([w[0][...], w[1][...]], axis=1)
                    wb_ref[...] = wide[:, shift:shift + tn].astype(BF16)
                else:
                    wb_ref[...] = w[0][...].astype(BF16)

    def finish(rows, vals):
        if n_w == 2:
            res = _silu(vals[0]) * vals[1]
        else:
            res = vals[0]
        if has_bias:
            res = res + b_ref[...]
        o_ref[0:rows, :] = res.astype(o_ref.dtype)

    def body(rows):
        x = x_ref[0:rows, :]
        vals = [jnp.dot(x, wb_ref[...], preferred_element_type=F32) for wb_ref in wb_refs]
        if nk == 1:
            finish(rows, vals)
            return

        @pl.when(k == 0)
        def _():
            acc_ref[ml, 0:rows, :] = vals[0]

        @pl.when(jnp.logical_and(k > 0, k < nk - 1))
        def _():
            acc_ref[ml, 0:rows, :] += vals[0]

        @pl.when(k == nk - 1)
        def _():
            finish(rows, [acc_ref[ml, 0:rows, :] + vals[0]])

    if grouped:
        @pl.when(tv_ref[m] == 1)
        def _():
            body(tm)

        @pl.when(jnp.logical_and(tv_ref[m] == 0, k == nk - 1))
        def _():
            o_ref[...] = jnp.zeros_like(o_ref)
    elif tail == tm:
        body(tm)
    else:
        @pl.when(ml < nm - 1)
        def _():
            body(tm)

        @pl.when(ml == nm - 1)
        def _():
            body(tail)


def matmul(x, ws, bias=None, *, out_dtype=F32, tm=1024, tn=512, tk=None, groups=None, lead=None,
           col0=0, n_cols=None, row_splits=1, single_buffer_w=False, name="matmul"):
    ws = list(ws)
    n_w = len(ws)
    grouped = groups is not None
    three_d = ws[0].ndim == 3
    assert three_d == (grouped or lead is not None)
    r, kdim = x.shape
    n_total = ws[0].shape[-1]
    ndim = n_total - col0 if n_cols is None else n_cols
    cast_w = ws[0].dtype != BF16
    tm = _pick_tile(r, tm, 8)
    nm_all = pl.cdiv(r, tm)
    assert nm_all % row_splits == 0 and (row_splits == 1 or r % tm == 0)
    nm = nm_all // row_splits
    tail = r - (nm_all - 1) * tm
    if grouped:
        assert tail == tm
    tn = _pick_tile(ndim, tn, 128)
    nn = pl.cdiv(ndim, tn)
    shift = col0 % 128
    base = col0 - shift
    assert base % tn == 0 and (shift == 0 or (cast_w and tn % 128 == 0 and ndim % tn == 0))
    tk = kdim if tk is None else tk
    assert kdim % tk == 0
    nk = kdim // tk
    assert nk == 1 or n_w == 1

    def row_of(s, k, ml):
        m = s * nm + ml
        return m if nk == 1 else jnp.where(k == nk - 1, m, s * nm)

    def widx(args, blk, col_block):
        s, n, k, ml = args[:4]
        if grouped:
            return (args[4][s * nm + ml], k, col_block(n))
        if lead is not None:
            return (lead, k, col_block(n))
        return (k, col_block(n))

    wshape = lambda width: ((None, tk, width) if three_d else (tk, width))
    w_mode = dict(pipeline_mode=pl.Buffered(1)) if single_buffer_w else {}
    w_specs = [pl.BlockSpec(wshape(tn), lambda *a: widx(a, tn, lambda n: base // tn + n), **w_mode)]
    if shift:
        w_specs.append(pl.BlockSpec(wshape(128), lambda *a: widx(a, 128, lambda n: (base + (n + 1) * tn) // 128),
                                    **w_mode))

    in_specs = [pl.BlockSpec((tm, tk), lambda *a: (a[0] * nm + a[3], a[2]))]
    ins = [x]
    for w in ws:
        in_specs += w_specs
        ins += [w] * len(w_specs)
    if bias is not None:
        in_specs.append(pl.BlockSpec((1, tn), lambda *a: (0, a[1])))
        ins.append(bias.reshape(1, ndim).astype(F32))
    scratch = []
    if cast_w:
        scratch += [pltpu.VMEM((tk, tn), BF16)] * n_w
    if nk > 1:
        scratch.append(pltpu.VMEM((nm, tm, tn), F32))
    kern = functools.partial(_mm_kernel, n_w=n_w, has_bias=bias is not None, nk=nk, nm=nm, tm=tm,
                             tail=tail, cast_w=cast_w, grouped=grouped, shift=shift, tn=tn)
    grid_spec = pltpu.PrefetchScalarGridSpec(
        num_scalar_prefetch=3 if grouped else 0,
        grid=(row_splits, nn, nk, nm),
        in_specs=in_specs,
        out_specs=pl.BlockSpec((tm, tn), lambda *a: (row_of(a[0], a[2], a[3]), a[1])),
        scratch_shapes=scratch,
    )
    call = pl.pallas_call(
        kern, grid_spec=grid_spec,
        out_shape=jax.ShapeDtypeStruct((r, ndim), out_dtype),
        compiler_params=_cparams(("arbitrary",) * 4),
        name=name,
    )
    if grouped:
        return call(*groups, *ins)
    return call(*ins)


def _mlstm_kernel(qz_ref, kz_ref, vz_ref, oz_ref, li_ref, lf_ref, cwq_ref, cwk_ref, cbq_ref, cbk_ref,
                  nw_ref, tq_ref, tk_ref, c0_ref, n0_ref, m0_ref,
                  h_ref, cn_ref, nn_ref, mn_ref,
                  prevq, prevk, c_s, n_s, m_s, *, L, nc, t_valid):
    c = pl.program_id(2)

    @pl.when(c == 0)
    def _():
        prevq[...] = jnp.zeros_like(prevq)
        prevk[...] = jnp.zeros_like(prevk)
        prevq[L - 8:L, :] = tq_ref[0]
        prevk[L - 8:L, :] = tk_ref[0]
        c_s[...] = c0_ref[0, 0]
        n_s[...] = n0_ref[0]
        m_s[...] = m0_ref[0]

    row = lax.broadcasted_iota(jnp.int32, (L, HEAD_DIM), 0)

    def conv(u, prev, w_ref, b_ref):
        out = b_ref[...] + w_ref[CONV_W - 1:CONV_W, :] * u
        for j in range(1, CONV_W):
            shifted = jnp.where(row < j, pltpu.roll(prev, j, 0), pltpu.roll(u, j, 0))
            out = out + w_ref[CONV_W - 1 - j:CONV_W - j, :] * shifted
        return out

    uq = qz_ref[...]
    uk = kz_ref[...]
    q = _silu(conv(uq, prevq[...], cwq_ref, cbq_ref))
    k = _silu(conv(uk, prevk[...], cwk_ref, cbk_ref)) * (HEAD_DIM ** -0.5)
    prevq[...] = uq
    prevk[...] = uk
    v = vz_ref[...]

    li_row = li_ref[0]
    lf_row = _log_sigmoid(lf_ref[0])
    if t_valid is not None:
        lane_t = c * L + lax.broadcasted_iota(jnp.int32, (1, L), 1)
        li_row = jnp.where(lane_t < t_valid, li_row, NEG_INF)
        lf_row = jnp.where(lane_t < t_valid, lf_row, 0.0)

    ri = lax.broadcasted_iota(jnp.int32, (L, L), 0)
    ci = lax.broadcasted_iota(jnp.int32, (L, L), 1)
    causal = ci <= ri
    b_col = jnp.sum(jnp.where(causal, lf_row, 0.0), axis=1, keepdims=True)
    b_row = _col_to_row(b_col, L)
    li_col = _row_to_col(li_row, L)
    m_prev = m_s[...]
    c_prev = c_s[...]
    n_prev = n_s[...]

    logw = jnp.where(causal, b_col - b_row + li_row, NEG_INF)
    g_col = b_col + m_prev
    m_t = jnp.maximum(jnp.max(logw, axis=1, keepdims=True), g_col)
    qb = q.astype(BF16)
    scores = _dot_nt(qb, k.astype(BF16)) * jnp.exp(logw - m_t)
    inter = jnp.exp(g_col - m_t)
    vb = v.astype(BF16)
    num = (jnp.dot(scores.astype(BF16), vb, preferred_element_type=F32)
           + inter * jnp.dot(qb, c_prev.astype(BF16), preferred_element_type=F32))
    den = jnp.sum(scores, axis=1, keepdims=True) + inter * jnp.sum(q * n_prev, axis=1, keepdims=True)
    h = num / jnp.maximum(jnp.abs(den), jnp.exp(-m_t))

    m_new = m_t[L - 1:L, :]
    b_last = b_col[L - 1:L, :]
    wk_col = jnp.exp(b_last - b_col + li_col - m_new)
    decay = jnp.exp(b_last + m_prev - m_new)
    kw = k * wk_col
    c_new = decay * c_prev + jnp.dot(kw.T.astype(BF16), vb, preferred_element_type=F32)
    n_new = decay * n_prev + jnp.sum(kw, axis=0, keepdims=True)
    c_s[...] = c_new
    n_s[...] = n_new
    m_s[...] = m_new

    hn = _rms(h, nw_ref[...])
    h_ref[...] = (hn * _sigmoid(oz_ref[...])).astype(h_ref.dtype)

    @pl.when(c == nc - 1)
    def _():
        cn_ref[0, 0] = c_new
        nn_ref[0] = n_new
        mn_ref[0] = m_new


def mlstm_group(z, gates_t, conv_w, conv_b, norm_w, conv_tail, c0, n0, m0, *, bsz, t, n_heads, t_valid):
    L = 128
    nc = t // L
    gpb = gates_t.shape[0] // bsz
    h = n_heads
    zspec = lambda off: pl.BlockSpec((L, HEAD_DIM), lambda b, hh, c: (b * nc + c, off + hh))
    gspec = lambda off: pl.BlockSpec((1, 1, L), lambda b, hh, c: (b * gpb + off + hh, 0, c))
    wspec = lambda off: pl.BlockSpec((CONV_W, HEAD_DIM), lambda b, hh, c: (0, off + hh))
    bspec = lambda off: pl.BlockSpec((1, HEAD_DIM), lambda b, hh, c: (0, off + hh))
    tspec = lambda off: pl.BlockSpec((1, 8, HEAD_DIM), lambda b, hh, c: (b, 0, off + hh))
    in_specs = [zspec(0), zspec(h), zspec(2 * h), zspec(3 * h), gspec(0), gspec(h),
                wspec(0), wspec(h), bspec(0), bspec(h),
                pl.BlockSpec((1, HEAD_DIM), lambda b, hh, c: (0, 0)),
                tspec(0), tspec(h),
                pl.BlockSpec((1, 1, HEAD_DIM, HEAD_DIM), lambda b, hh, c: (b, hh, 0, 0)),
                pl.BlockSpec((1, 1, HEAD_DIM), lambda b, hh, c: (b * h + hh, 0, 0)),
                pl.BlockSpec((1, 1, 1), lambda b, hh, c: (b * h + hh, 0, 0))]
    out_specs = [pl.BlockSpec((L, HEAD_DIM), lambda b, hh, c: (b * nc + c, hh)),
                 pl.BlockSpec((1, 1, HEAD_DIM, HEAD_DIM), lambda b, hh, c: (b, hh, 0, 0)),
                 pl.BlockSpec((1, 1, HEAD_DIM), lambda b, hh, c: (b * h + hh, 0, 0)),
                 pl.BlockSpec((1, 1, 1), lambda b, hh, c: (b * h + hh, 0, 0))]
    out_shape = [jax.ShapeDtypeStruct((bsz * t, h * HEAD_DIM), BF16),
                 jax.ShapeDtypeStruct((bsz, h, HEAD_DIM, HEAD_DIM), F32),
                 jax.ShapeDtypeStruct((bsz * h, 1, HEAD_DIM), F32),
                 jax.ShapeDtypeStruct((bsz * h, 1, 1), F32)]
    scratch = [pltpu.VMEM((L, HEAD_DIM), F32), pltpu.VMEM((L, HEAD_DIM), F32),
               pltpu.VMEM((HEAD_DIM, HEAD_DIM), F32), pltpu.VMEM((1, HEAD_DIM), F32),
               pltpu.VMEM((1, 1), F32)]
    outs = pl.pallas_call(
        functools.partial(_mlstm_kernel, L=L, nc=nc, t_valid=t_valid),
        grid=(bsz, h, nc), in_specs=in_specs, out_specs=out_specs, out_shape=out_shape,
        scratch_shapes=scratch,
        compiler_params=_cparams(("parallel", "parallel", "arbitrary")),
        name="mlstm_group",
    )(z, z, z, z, gates_t, gates_t, conv_w, conv_w, conv_b, conv_b, norm_w.reshape(1, HEAD_DIM),
      conv_tail, conv_tail, c0, n0.reshape(bsz * h, 1, HEAD_DIM), m0.reshape(bsz * h, 1, 1))
    return (outs[0], outs[1], outs[2].reshape(bsz, h, HEAD_DIM), outs[3].reshape(bsz, h))


def _hgrn2_kernel(qz_ref, fz_ref, iz_ref, gz_ref, lb_ref, nw_ref, s0_ref, o_ref, sn_ref, s_s,
                  *, L, nc, t_valid):
    c = pl.program_id(2)

    @pl.when(c == 0)
    def _():
        s_s[...] = s0_ref[0, 0]

    lb = lb_ref[...]
    q = _silu(qz_ref[...])
    fg = lb + (1.0 - lb) * _sigmoid(fz_ref[...])
    k = 1.0 - fg
    lf = jnp.log(fg)
    v = iz_ref[...]
    if t_valid is not None:
        t_idx = c * L + lax.broadcasted_iota(jnp.int32, (L, HEAD_DIM), 0)
        lf = jnp.where(t_idx < t_valid, lf, 0.0)
        k = jnp.where(t_idx < t_valid, k, 0.0)

    ri = lax.broadcasted_iota(jnp.int32, (L, L), 0)
    ci = lax.broadcasted_iota(jnp.int32, (L, L), 1)
    tril = jnp.where(ci <= ri, 1.0, 0.0).astype(F32)
    b = jnp.dot(tril, lf, preferred_element_type=F32, precision=HIGHEST)
    s_prev = s_s[...]
    vb = v.astype(BF16)
    o_inter = jnp.dot((q * jnp.exp(b)).astype(BF16), s_prev.astype(BF16), preferred_element_type=F32)

    srow = lax.broadcasted_iota(jnp.int32, (L, HEAD_DIM), 0)
    trow = lax.broadcasted_iota(jnp.int32, (SUB, HEAD_DIM), 0)
    pieces = []
    for blk in range(L // SUB):
        r0 = blk * SUB
        q_i = q[r0:r0 + SUB]
        b_i = b[r0:r0 + SUB]
        o_i = o_inter[r0:r0 + SUB]
        if blk > 0:
            b_mid = b[r0 - 1:r0]
            qp = q_i * jnp.exp(b_i - b_mid)
            kp = k * jnp.exp(jnp.where(srow < r0, b_mid - b, NEG_INF))
            a = _dot_nt(qp.astype(BF16), kp.astype(BF16))
            o_i = o_i + jnp.dot(a.astype(BF16), vb, preferred_element_type=F32)
        for s in range(SUB):
            w = jnp.exp(jnp.where(trow >= s, b_i - b_i[s:s + 1], NEG_INF))
            a_col = jnp.sum(q_i * k[r0 + s:r0 + s + 1] * w, axis=1, keepdims=True)
            o_i = o_i + a_col * v[r0 + s:r0 + s + 1]
        pieces.append(o_i)
    o = jnp.concatenate(pieces, axis=0)

    b_last = b[L - 1:L]
    kw = k * jnp.exp(b_last - b)
    s_new = (_row_to_col(jnp.exp(b_last), HEAD_DIM) * s_prev
             + jnp.dot(kw.T.astype(BF16), vb, preferred_element_type=F32))
    s_s[...] = s_new

    o_ref[...] = (_rms(o, nw_ref[...]) * _silu(gz_ref[...])).astype(o_ref.dtype)

    @pl.when(c == nc - 1)
    def _():
        sn_ref[0, 0] = s_new


def hgrn2_group(z, col0, lb, norm_w, s0, *, bsz, t, n_heads, t_valid):
    L = 128
    nc = t // L
    h = n_heads
    zspec = lambda off: pl.BlockSpec((L, HEAD_DIM), lambda b, hh, c: (b * nc + c, col0 + off + hh))
    sspec = pl.BlockSpec((1, 1, HEAD_DIM, HEAD_DIM), lambda b, hh, c: (b, hh, 0, 0))
    outs = pl.pallas_call(
        functools.partial(_hgrn2_kernel, L=L, nc=nc, t_valid=t_valid),
        grid=(bsz, h, nc),
        in_specs=[zspec(0), zspec(h), zspec(2 * h), zspec(3 * h),
                  pl.BlockSpec((1, HEAD_DIM), lambda b, hh, c: (0, hh)),
                  pl.BlockSpec((1, HEAD_DIM), lambda b, hh, c: (0, 0)),
                  sspec],
        out_specs=[pl.BlockSpec((L, HEAD_DIM), lambda b, hh, c: (b * nc + c, hh)), sspec],
        out_shape=[jax.ShapeDtypeStruct((bsz * t, h * HEAD_DIM), BF16),
                   jax.ShapeDtypeStruct((bsz, h, HEAD_DIM, HEAD_DIM), F32)],
        scratch_shapes=[pltpu.VMEM((HEAD_DIM, HEAD_DIM), F32)],
        compiler_params=_cparams(("parallel", "parallel", "arbitrary")),
        name="hgrn2_group",
    )(z, z, z, z, lb, norm_w.reshape(1, HEAD_DIM), s0)
    return outs[0], outs[1]


def _fox_gate_kernel(g_ref, lf_ref, cum_ref, *, t):
    lf = _log_sigmoid(g_ref[0])
    lf_ref[0] = lf
    ri = lax.broadcasted_iota(jnp.int32, (128, 128), 0)
    ci = lax.broadcasted_iota(jnp.int32, (128, 128), 1)
    upper = jnp.where(ri <= ci, 1.0, 0.0).astype(F32)
    carry = jnp.zeros((lf.shape[0], 1), F32)
    for ch in range(t // 128):
        seg = lf[:, ch * 128:(ch + 1) * 128]
        cs = jnp.dot(seg, upper, preferred_element_type=F32, precision=HIGHEST) + carry
        cum_ref[0, :, ch * 128:(ch + 1) * 128] = cs
        carry = cs[:, 127:128]


def fox_gates(gates_t, row_block, n_heads):
    bsz, _, t = gates_t.shape
    spec = pl.BlockSpec((1, n_heads, t), lambda b: (b, 0, 0))
    return pl.pallas_call(
        functools.partial(_fox_gate_kernel, t=t),
        grid=(bsz,),
        in_specs=[pl.BlockSpec((1, n_heads, t), lambda b: (b, row_block, 0))],
        out_specs=[spec, spec],
        out_shape=[jax.ShapeDtypeStruct((bsz, n_heads, t), F32)] * 2,
        compiler_params=_cparams(("parallel",)),
        name="fox_gates",
    )(gates_t)


def _fox_prefill_kernel(q_ref, k_ref, v_ref, fq_ref, fk_ref, o_ref, *, tq, tk):
    qi = pl.program_id(2)
    q = q_ref[...].astype(BF16)
    fq_col = _row_to_col(fq_ref[0], tq)
    qpos = qi * tq + lax.broadcasted_iota(jnp.int32, (tq, tk), 0)
    kloc = lax.broadcasted_iota(jnp.int32, (tq, tk), 1)
    scale = HEAD_DIM ** -0.5

    def step(ki, carry):
        m, l, acc = carry
        start = pl.multiple_of(ki * tk, tk)
        kb = k_ref[pl.ds(start, tk), :].astype(BF16)
        vb = v_ref[pl.ds(start, tk), :].astype(BF16)
        s = _dot_nt(q, kb) * scale + fq_col - fk_ref[0, ki]
        s = jnp.where(ki * tk + kloc <= qpos, s, NEG_INF)
        m_new = jnp.maximum(m, jnp.max(s, axis=1, keepdims=True))
        corr = jnp.exp(m - m_new)
        p = jnp.exp(s - m_new)
        l = l * corr + jnp.sum(p, axis=1, keepdims=True)
        acc = acc * corr + jnp.dot(p.astype(BF16), vb, preferred_element_type=F32)
        return m_new, l, acc

    init = (jnp.full((tq, 1), NEG_INF, F32), jnp.zeros((tq, 1), F32), jnp.zeros((tq, HEAD_DIM), F32))
    n_blocks = (qi * tq + tq + tk - 1) // tk
    _, l, acc = lax.fori_loop(0, n_blocks, step, init)
    o_ref[...] = (acc / l).astype(o_ref.dtype)


def fox_prefill(z, col0, f_cum, *, bsz, t, n_heads):
    tq = min(256, t)
    tk = min(512, t)
    h = n_heads
    nq = t // tq
    fq = f_cum.reshape(bsz * h, 1, t)
    fk = f_cum.reshape(bsz * h, t // tk, 1, tk)
    return pl.pallas_call(
        functools.partial(_fox_prefill_kernel, tq=tq, tk=tk),
        grid=(bsz, h, nq),
        in_specs=[pl.BlockSpec((tq, HEAD_DIM), lambda b, hh, i: (b * nq + i, col0 + hh)),
                  pl.BlockSpec((t, HEAD_DIM), lambda b, hh, i: (b, col0 + h + hh)),
                  pl.BlockSpec((t, HEAD_DIM), lambda b, hh, i: (b, col0 + 2 * h + hh)),
                  pl.BlockSpec((1, 1, tq), lambda b, hh, i: (b * h + hh, 0, i)),
                  pl.BlockSpec((1, t // tk, 1, tk), lambda b, hh, i: (b * h + hh, 0, 0, 0))],
        out_specs=pl.BlockSpec((tq, HEAD_DIM), lambda b, hh, i: (b * nq + i, hh)),
        out_shape=jax.ShapeDtypeStruct((bsz * t, h * HEAD_DIM), BF16),
        compiler_params=_cparams(("parallel", "parallel", "arbitrary")),
        name="fox_prefill",
    )(z, z, z, fq, fk)


def _fox_decode_kernel(pt_ref, qt_ref, kp_ref, vp_ref, lfe_ref, grow_ref, knew_ref, vnew_ref, bnew_ref,
                       o_ref, m_s, l_s, acc_s, carry_s, *, n_pages, n_heads, page, dec_seq):
    j = pl.program_id(1)
    rows = page * n_heads
    cols = n_heads * 8

    @pl.when(j == 0)
    def _():
        m_s[...] = jnp.full_like(m_s, NEG_INF)
        l_s[...] = jnp.zeros_like(l_s)
        acc_s[...] = jnp.zeros_like(acc_s)
        carry_s[...] = jnp.zeros_like(carry_s)

    qt = qt_ref[0]
    r_i = lax.broadcasted_iota(jnp.int32, (rows, cols), 0)
    c_i = lax.broadcasted_iota(jnp.int32, (rows, cols), 1)
    same_head = lax.rem(r_i, n_heads) == lax.div(c_i, 8)
    src_of_row = lax.div(lax.broadcasted_iota(jnp.int32, (rows, page), 0), n_heads)
    sel = jnp.where(src_of_row == lax.broadcasted_iota(jnp.int32, (rows, page), 1), 1.0, 0.0).astype(BF16)

    def update(k3_ref, v3_ref, bias, mask):
        k2 = k3_ref[...].reshape(rows, HEAD_DIM).astype(BF16)
        v2 = v3_ref[...].reshape(rows, HEAD_DIM).astype(BF16)
        hi = bias.astype(BF16)
        rem = bias - hi.astype(F32)
        mid = rem.astype(BF16)
        lo = (rem - mid.astype(F32)).astype(BF16)
        s = (jnp.dot(k2, qt, preferred_element_type=F32) + jnp.dot(sel, hi, preferred_element_type=F32)
             + jnp.dot(sel, mid, preferred_element_type=F32) + jnp.dot(sel, lo, preferred_element_type=F32))
        s = jnp.where(mask, s, NEG_INF)
        m_old = m_s[...]
        m_new = jnp.maximum(m_old, jnp.max(s, axis=0, keepdims=True))
        corr = jnp.exp(m_old - m_new)
        p = jnp.exp(s - m_new)
        l_s[...] = l_s[...] * corr + jnp.sum(p, axis=0, keepdims=True)
        pv = jnp.dot(p.T.astype(BF16), v2, preferred_element_type=F32)
        acc_s[...] = acc_s[...] * _row_to_col(corr, cols) + pv
        m_s[...] = m_new

    lfe = lfe_ref[...]
    ri = lax.broadcasted_iota(jnp.int32, (page, page), 0)
    ci = lax.broadcasted_iota(jnp.int32, (page, page), 1)
    later = jnp.where(ci > ri, 1.0, 0.0).astype(F32)
    suffix = jnp.dot(later, lfe, preferred_element_type=F32, precision=HIGHEST) + carry_s[...]
    update(kp_ref, vp_ref, suffix + grow_ref[0], same_head)
    carry_s[...] = carry_s[...] + jnp.sum(lfe, axis=0, keepdims=True)

    @pl.when(j == n_pages - 1)
    def _():
        causal = jnp.logical_and(lax.div(r_i, n_heads) <= lax.rem(c_i, 8), lax.div(r_i, n_heads) < dec_seq)
        update(knew_ref.at[0], vnew_ref.at[0], bnew_ref[0], jnp.logical_and(same_head, causal))
        o_ref[0] = (acc_s[...] / _row_to_col(l_s[...], cols)).astype(o_ref.dtype)


def fox_decode(qt, k_pool, v_pool, lf_exp, layer, page_table, g_row, k_new, v_new, b_new, *, n_heads, dec_seq):
    bsz, n_pages = page_table.shape
    page = k_pool.shape[2]
    cols = n_heads * 8
    pool_map = lambda b, j, pt: (layer, pt[b, n_pages - 1 - j], 0, 0, 0)
    per_b3 = lambda b, j, pt: (b, 0, 0)
    per_b4 = lambda b, j, pt: (b, 0, 0, 0)
    grid_spec = pltpu.PrefetchScalarGridSpec(
        num_scalar_prefetch=1,
        grid=(bsz, n_pages),
        in_specs=[pl.BlockSpec((1, HEAD_DIM, cols), per_b3),
                  pl.BlockSpec((None, None, page, n_heads, HEAD_DIM), pool_map),
                  pl.BlockSpec((None, None, page, n_heads, HEAD_DIM), pool_map),
                  pl.BlockSpec((None, None, page, cols), lambda b, j, pt: (layer, pt[b, n_pages - 1 - j], 0, 0)),
                  pl.BlockSpec((1, 1, cols), per_b3),
                  pl.BlockSpec((1, page, n_heads, HEAD_DIM), per_b4),
                  pl.BlockSpec((1, page, n_heads, HEAD_DIM), per_b4),
                  pl.BlockSpec((1, page, cols), per_b3)],
        out_specs=pl.BlockSpec((1, cols, HEAD_DIM), per_b3),
        scratch_shapes=[pltpu.VMEM((1, cols), F32), pltpu.VMEM((1, cols), F32),
                        pltpu.VMEM((cols, HEAD_DIM), F32), pltpu.VMEM((1, cols), F32)],
    )
    return pl.pallas_call(
        functools.partial(_fox_decode_kernel, n_pages=n_pages, n_heads=n_heads, page=page, dec_seq=dec_seq),
        grid_spec=grid_spec,
        out_shape=jax.ShapeDtypeStruct((bsz, cols, HEAD_DIM), BF16),
        compiler_params=_cparams(("parallel", "arbitrary")),
        name="fox_decode",
    )(page_table, qt, k_pool, v_pool, lf_exp, g_row, k_new, v_new, b_new)


def _cross_attn_kernel(q_ref, k_ref, v_ref, o_ref):
    s = _dot_nt(q_ref[0].astype(BF16), k_ref[0].astype(BF16)) * (HEAD_DIM ** -0.5)
    m = jnp.max(s, axis=1, keepdims=True)
    p = jnp.exp(s - m)
    o = jnp.dot(p.astype(BF16), v_ref[0].astype(BF16), preferred_element_type=F32)
    o_ref[0] = (o / jnp.sum(p, axis=1, keepdims=True)).astype(o_ref.dtype)


def cross_attn(q, mem_k, mem_v):
    bsz, t, w = q.shape
    n_mem = mem_k.shape[1]
    h = w // HEAD_DIM
    tq = _pick_tile(t, 512, 8)
    return pl.pallas_call(
        _cross_attn_kernel,
        grid=(bsz, h, t // tq),
        in_specs=[pl.BlockSpec((1, tq, HEAD_DIM), lambda b, hh, i: (b, i, hh)),
                  pl.BlockSpec((1, n_mem, HEAD_DIM), lambda b, hh, i: (b, 0, hh)),
                  pl.BlockSpec((1, n_mem, HEAD_DIM), lambda b, hh, i: (b, 0, hh))],
        out_specs=pl.BlockSpec((1, tq, HEAD_DIM), lambda b, hh, i: (b, i, hh)),
        out_shape=jax.ShapeDtypeStruct((bsz, t, w), BF16),
        compiler_params=_cparams(("parallel", "parallel", "parallel")),
        name="cross_attn",
    )(q, mem_k, mem_v)


def _router_kernel(x_ref, g_ref, wr_ref, br_ref, o_ref, *, n_experts):
    xn = _rms(x_ref[...], g_ref[...])
    logits = lax.dot_general(wr_ref[...], xn, (((1,), (1,)), ((), ())),
                             preferred_element_type=F32, precision=HIGHEST) + br_ref[...]
    e_idx = lax.broadcasted_iota(jnp.int32, logits.shape, 0)
    v1 = jnp.max(logits, axis=0, keepdims=True)
    i1 = jnp.min(jnp.where(logits == v1, e_idx, n_experts), axis=0, keepdims=True)
    rest = jnp.where(e_idx == i1, NEG_INF, logits)
    v2 = jnp.max(rest, axis=0, keepdims=True)
    i2 = jnp.min(jnp.where(rest == v2, e_idx, n_experts), axis=0, keepdims=True)
    e2 = jnp.exp(v2 - v1)
    g1 = 1.0 / (1.0 + e2)
    g2 = e2 / (1.0 + e2)
    zeros = jnp.zeros((4, logits.shape[1]), F32)
    o_ref[...] = jnp.concatenate([i1.astype(F32), i2.astype(F32), g1, g2, zeros], axis=0)


def route_top2(x, g_pre, w_router, b_router):
    r, d = x.shape
    n_experts = w_router.shape[1]
    tr = _pick_tile(r, 256, 128)
    return pl.pallas_call(
        functools.partial(_router_kernel, n_experts=n_experts),
        grid=(pl.cdiv(r, tr),),
        in_specs=[pl.BlockSpec((tr, d), lambda i: (i, 0)),
                  pl.BlockSpec((1, d), lambda i: (0, 0)),
                  pl.BlockSpec((n_experts, d), lambda i: (0, 0)),
                  pl.BlockSpec((n_experts, 1), lambda i: (0, 0))],
        out_specs=pl.BlockSpec((8, tr), lambda i: (0, i)),
        out_shape=jax.ShapeDtypeStruct((8, r), F32),
        compiler_params=_cparams(("parallel",)),
        name="route_top2",
    )(x, g_pre.reshape(1, d).astype(F32), w_router.T.astype(F32), b_router.reshape(n_experts, 1).astype(F32))


def _dispatch_kernel(src_ref, x_hbm, g_ref, o_ref, buf, sem, *, tg):
    base = pl.program_id(0) * tg

    def row_copy(j):
        return pltpu.make_async_copy(x_hbm.at[pl.ds(src_ref[base + j], 1)], buf.at[pl.ds(j, 1)], sem)

    def start(j, carry):
        row_copy(j).start()
        return carry

    def wait(j, carry):
        row_copy(j).wait()
        return carry

    lax.fori_loop(0, tg, start, 0)
    lax.fori_loop(0, tg, wait, 0)
    o_ref[...] = _rms(buf[...], g_ref[...]).astype(o_ref.dtype)


def dispatch_rows(x, g_pre, src_rows):
    r, d = x.shape
    n_out = src_rows.shape[0]
    tg = 256
    assert n_out % tg == 0
    grid_spec = pltpu.PrefetchScalarGridSpec(
        num_scalar_prefetch=1,
        grid=(n_out // tg,),
        in_specs=[pl.BlockSpec(memory_space=pl.ANY), pl.BlockSpec((1, d), lambda i, src: (0, 0))],
        out_specs=pl.BlockSpec((tg, d), lambda i, src: (i, 0)),
        scratch_shapes=[pltpu.VMEM((tg, d), F32), pltpu.SemaphoreType.DMA(())],
    )
    return pl.pallas_call(
        functools.partial(_dispatch_kernel, tg=tg),
        grid_spec=grid_spec,
        out_shape=jax.ShapeDtypeStruct((n_out, d), BF16),
        compiler_params=_cparams(("arbitrary",)),
        name="dispatch_rows",
    )(src_rows, x, g_pre.reshape(1, d).astype(F32))


def _combine_kernel(p1_ref, p2_ref, y_hbm, g1_ref, g2_ref, o_ref, buf1, buf2, sem, *, tc):
    base = pl.program_id(0) * tc

    def copies(j):
        return (pltpu.make_async_copy(y_hbm.at[pl.ds(p1_ref[base + j], 1)], buf1.at[pl.ds(j, 1)], sem.at[0]),
                pltpu.make_async_copy(y_hbm.at[pl.ds(p2_ref[base + j], 1)], buf2.at[pl.ds(j, 1)], sem.at[1]))

    def start(j, carry):
        for cp in copies(j):
            cp.start()
        return carry

    def wait(j, carry):
        for cp in copies(j):
            cp.wait()
        return carry

    lax.fori_loop(0, tc, start, 0)
    lax.fori_loop(0, tc, wait, 0)
    g1 = _row_to_col(g1_ref[0], tc)
    g2 = _row_to_col(g2_ref[0], tc)
    o_ref[...] = g1 * buf1[...] + g2 * buf2[...]


def combine_rows(y, pos1, pos2, gate1, gate2, n_rows):
    d = y.shape[1]
    tc = 128
    n_pad = pos1.shape[0]
    assert n_pad % tc == 0 and n_pad >= n_rows
    grid_spec = pltpu.PrefetchScalarGridSpec(
        num_scalar_prefetch=2,
        grid=(n_pad // tc,),
        in_specs=[pl.BlockSpec(memory_space=pl.ANY),
                  pl.BlockSpec((1, 1, tc), lambda i, p1, p2: (i, 0, 0)),
                  pl.BlockSpec((1, 1, tc), lambda i, p1, p2: (i, 0, 0))],
        out_specs=pl.BlockSpec((tc, d), lambda i, p1, p2: (i, 0)),
        scratch_shapes=[pltpu.VMEM((tc, d), F32), pltpu.VMEM((tc, d), F32), pltpu.SemaphoreType.DMA((2,))],
    )
    return pl.pallas_call(
        functools.partial(_combine_kernel, tc=tc),
        grid_spec=grid_spec,
        out_shape=jax.ShapeDtypeStruct((n_rows, d), F32),
        compiler_params=_cparams(("arbitrary",)),
        name="combine_rows",
    )(pos1, pos2, y, gate1.reshape(n_pad // tc, 1, tc), gate2.reshape(n_pad // tc, 1, tc))


MOE_TM = 512
MOE_ROW_SPLITS = 2


def moe_ffn(x, g_pre, w_router, b_router, w1, w3, w2):
    r, d = x.shape
    n_experts = w1.shape[0]
    d_ff = w1.shape[2]
    tm = MOE_TM
    routed = route_top2(x, g_pre, w_router, b_router)
    idx = routed[0:2].astype(jnp.int32)
    gates = routed[2:4]

    e_flat = idx.reshape(-1)
    onehot = (e_flat[:, None] == jnp.arange(n_experts, dtype=jnp.int32)[None, :]).astype(jnp.int32)
    rank = jnp.sum((jnp.cumsum(onehot, axis=0) - onehot) * onehot, axis=1)
    counts = jnp.sum(onehot, axis=0)
    tiles = (counts + tm - 1) // tm
    tile_end = jnp.cumsum(tiles)
    tile_start = tile_end - tiles
    pos = (tile_start * tm)[e_flat] + rank
    n_tiles = _round_up((TOP_K * r + n_experts * (tm - 1)) // tm + 1, MOE_ROW_SPLITS)
    token = jnp.tile(jnp.arange(r, dtype=jnp.int32), TOP_K)
    src_rows = jnp.zeros((n_tiles * tm,), jnp.int32).at[pos].set(token)
    t_idx = jnp.arange(n_tiles, dtype=jnp.int32)
    used = tile_end[-1]
    t_clamped = jnp.minimum(t_idx, used - 1)
    tile_expert = jnp.sum((t_clamped[:, None] >= tile_end[None, :]).astype(jnp.int32), axis=1)
    tile_valid = (t_idx < used).astype(jnp.int32)
    tile_first = jnp.logical_and(tile_valid == 1,
                                 jnp.any(t_idx[:, None] == tile_start[None, :], axis=1)).astype(jnp.int32)
    tile_first = tile_first.at[0].set(1)
    groups = (tile_expert.astype(jnp.int32), tile_first, tile_valid)

    xs = dispatch_rows(x, g_pre, src_rows)
    hid = matmul(xs, [w1, w3], out_dtype=BF16, tm=tm, tn=512, groups=groups, name="moe_up")
    y = matmul(hid, [w2], tm=tm, tn=512, tk=_div_tile(d_ff, 3584), groups=groups,
               row_splits=MOE_ROW_SPLITS, name="moe_down")

    tc = 128
    n_pad = _round_up(r, tc)
    padi = lambda a: jnp.pad(a, (0, n_pad - r))
    return combine_rows(y, padi(pos[:r]), padi(pos[r:]), padi(gates[0]), padi(gates[1]), r)


def kernel(x_prompt, x_sample, cache_fox_k, cache_fox_v, cache_fox_logf, cache_mem_k, cache_mem_v,
           state_ml_conv, state_ml_C, state_ml_n, state_ml_m, state_hg_S, page_table, mem_prompt,
           w_in, b_in, ml_conv_w, ml_conv_b, ml_norm_w, hg_lower_bounds, hg_norm_w, w_out,
           ca_mem_norm, ca_w_q, ca_w_k, ca_w_v, ca_w_o,
           norm_mix_pre, norm_mix_post, norm_ca_pre, norm_ca_post, norm_ffn_pre, norm_ffn_post,
           ffn_w1, ffn_w3, ffn_w2, moe_router, moe_router_b, moe_w1, moe_w3, moe_w2):
    bp, seq, d_model = x_prompt.shape
    bs, dec_seq, _ = x_sample.shape
    depth = w_in.shape[0]
    ml_h = state_ml_C.shape[2]
    hg_h = state_hg_S.shape[2]
    fx_h = cache_fox_k.shape[3]
    ml_w, hg_w, fx_w = ml_h * HEAD_DIM, hg_h * HEAD_DIM, fx_h * HEAD_DIM
    n_pool, page = cache_fox_k.shape[1], cache_fox_k.shape[2]
    n_mem = mem_prompt.shape[1]
    ca_w = ca_w_q.shape[2]
    ca_h = ca_w // HEAD_DIM
    n_p, n_s = bp * seq, bs * dec_seq
    assert dec_seq <= 8 and seq % 256 == 0 or seq == 128

    sizes = (2 * ml_w, ml_w, ml_w, ml_h, ml_h, hg_w, hg_w, hg_w, hg_w, fx_w, fx_w, fx_w, fx_h)
    starts = [0]
    for s in sizes:
        starts.append(starts[-1] + s)
    wide = [(0, starts[3]), (starts[5], starts[12])]
    narrow = [(starts[3], starts[5]), (starts[12], starts[13])]
    n_gate = sum(b - a for a, b in narrow)
    fx_col = 4 * hg_h

    lb_soft = jax.nn.softmax(hg_lower_bounds.astype(F32), axis=0)
    lower_bounds = jnp.cumsum(lb_soft, axis=0) - lb_soft[0]
    lf_exp = jnp.repeat(cache_fox_logf.astype(F32), 8, axis=-1)

    x = jnp.concatenate([x_prompt.reshape(n_p, d_model), x_sample.reshape(n_s, d_model)], axis=0)
    h = rmsnorm_rows(x, norm_mix_pre[0], BF16)

    l_ml = 128
    acc = {name: [] for name in ('p_k', 'p_v', 'p_lf', 'p_mk', 'p_mv', 'p_conv', 'p_C', 'p_n', 'p_m', 'p_S',
                                 's_k', 's_v', 's_lf', 's_conv', 's_C', 's_n', 's_m', 's_S')}
    gate_cols = lambda a: jnp.concatenate([a[..., s:e] for s, e in narrow], axis=-1)
    for l in range(depth):
        za = matmul(h, [w_in], b_in[l, wide[0][0]:wide[0][1]], tm=1024, tn=1024, lead=l, single_buffer_w=True,
                    col0=wide[0][0], n_cols=wide[0][1] - wide[0][0], name="in_proj_ml")
        zb = matmul(h, [w_in], b_in[l, wide[1][0]:wide[1][1]], tm=1024, tn=512, lead=l, single_buffer_w=True,
                    col0=wide[1][0], n_cols=wide[1][1] - wide[1][0], name="in_proj_hgfx")
        zg = matmul(h, [gate_cols(w_in[l])], gate_cols(b_in[l]), tm=1024, tn=128, name="in_proj_gates")
        pad_s = lambda a: jnp.pad(a[n_p:].reshape(bs, dec_seq, -1),
                                  ((0, 0), (0, l_ml - dec_seq), (0, 0)))
        za_s = pad_s(za).reshape(bs * l_ml, -1)
        zb_s = pad_s(zb).reshape(bs * l_ml, -1)
        g_p = zg[:n_p].reshape(bp, seq, n_gate).transpose(0, 2, 1)
        g_s = pad_s(zg).transpose(0, 2, 1)

        qk_p = za[:n_p, :2 * ml_w].reshape(bp, seq, 2 * ml_w)
        qk_s = za[n_p:, :2 * ml_w].reshape(bs, dec_seq, 2 * ml_w)
        conv_p = qk_p[:, seq - (CONV_W - 1):]
        conv_s = jnp.concatenate([state_ml_conv[l], qk_s], axis=1)[:, dec_seq:]
        tail_p = jnp.zeros((bp, 8, 2 * ml_w), F32)
        tail_s = jnp.pad(state_ml_conv[l], ((0, 0), (8 - (CONV_W - 1), 0), (0, 0)))

        lb_l = lower_bounds[l].reshape(1, hg_w)
        hml_p, c_p, nrm_p, m_p = mlstm_group(
            za, g_p.reshape(bp * n_gate, 1, seq), ml_conv_w[l], ml_conv_b[l].reshape(1, -1), ml_norm_w[l],
            tail_p, jnp.zeros((bp, ml_h, HEAD_DIM, HEAD_DIM), F32), jnp.zeros((bp, ml_h, HEAD_DIM), F32),
            jnp.zeros((bp, ml_h), F32), bsz=bp, t=seq, n_heads=ml_h, t_valid=None)
        hml_s, c_s, nrm_s, m_s = mlstm_group(
            za_s, g_s.reshape(bs * n_gate, 1, l_ml), ml_conv_w[l], ml_conv_b[l].reshape(1, -1), ml_norm_w[l],
            tail_s, state_ml_C[l], state_ml_n[l], state_ml_m[l], bsz=bs, t=l_ml, n_heads=ml_h, t_valid=dec_seq)
        ohg_p, s_p = hgrn2_group(zb, 0, lb_l, hg_norm_w[l], jnp.zeros((bp, hg_h, HEAD_DIM, HEAD_DIM), F32),
                                 bsz=bp, t=seq, n_heads=hg_h, t_valid=None)
        ohg_s, s_s = hgrn2_group(zb_s, 0, lb_l, hg_norm_w[l], state_hg_S[l],
                                 bsz=bs, t=l_ml, n_heads=hg_h, t_valid=dec_seq)
        fx_row_block = (2 * ml_h) // fx_h
        assert fx_row_block * fx_h == 2 * ml_h
        lf_p, fcum_p = fox_gates(g_p, fx_row_block, fx_h)
        lf_s, fcum_s = fox_gates(g_s, fx_row_block, fx_h)
        ofx_p = fox_prefill(zb, fx_col, fcum_p, bsz=bp, t=seq, n_heads=fx_h)

        fx0 = fx_col * HEAD_DIM
        q_s = zb[n_p:, fx0:fx0 + fx_w].reshape(bs, dec_seq, fx_h, HEAD_DIM)
        k_s = zb[n_p:, fx0 + fx_w:fx0 + 2 * fx_w].reshape(bs, dec_seq, fx_w)
        v_s = zb[n_p:, fx0 + 2 * fx_w:fx0 + 3 * fx_w].reshape(bs, dec_seq, fx_w)
        qt = jnp.pad(q_s * (HEAD_DIM ** -0.5), ((0, 0), (0, 8 - dec_seq), (0, 0), (0, 0)))
        qt = qt.transpose(0, 3, 2, 1).reshape(bs, HEAD_DIM, fx_h * 8).astype(BF16)
        g_tok = fcum_s[:, :, :8]
        g_row = jnp.where(jnp.arange(8)[None, None, :] < dec_seq, g_tok, 0.0).reshape(bs, 1, fx_h * 8)
        g_src = fcum_s[:, :, :page].transpose(0, 2, 1)
        b_new = (g_tok[:, None, :, :] - g_src[:, :, :, None]).reshape(bs, page, fx_h * 8)
        padk = lambda a: jnp.pad(a, ((0, 0), (0, page - dec_seq), (0, 0))).reshape(bs, page, fx_h, HEAD_DIM)
        ofx_s = fox_decode(qt, cache_fox_k, cache_fox_v, lf_exp, l, page_table, g_row, padk(k_s), padk(v_s),
                           b_new, n_heads=fx_h, dec_seq=dec_seq)
        ofx_s = ofx_s.reshape(bs, fx_h, 8, HEAD_DIM)[:, :, :dec_seq].transpose(0, 2, 1, 3).reshape(n_s, fx_w)

        unpad = lambda a: a.reshape(bs, l_ml, -1)[:, :dec_seq].reshape(n_s, -1)
        merged = jnp.concatenate([
            jnp.concatenate([hml_p, ohg_p, ofx_p], axis=1),
            jnp.concatenate([unpad(hml_s), unpad(ohg_s), ofx_s], axis=1)], axis=0)
        mix = matmul(merged, [w_out], tm=1024, tn=1024, lead=l, single_buffer_w=True, name="out_proj")
        x, h_ca = add_norm(x, mix, norm_mix_post[l], norm_ca_pre[l])

        hm = rmsnorm_rows(mem_prompt.reshape(bp * n_mem, d_model), ca_mem_norm[l], BF16)
        mk = matmul(hm, [ca_w_k], tm=1024, tn=512, lead=l, name="mem_k")
        mv = matmul(hm, [ca_w_v], tm=1024, tn=512, lead=l, name="mem_v")
        qc = matmul(h_ca, [ca_w_q], tm=1024, tn=512, lead=l, name="ca_q")
        o_p = cross_attn(qc[:n_p].reshape(bp, seq, ca_w), mk.reshape(bp, n_mem, ca_w), mv.reshape(bp, n_mem, ca_w))
        qc_s = jnp.pad(qc[n_p:].reshape(bs, dec_seq, ca_w), ((0, 0), (0, 8 - dec_seq), (0, 0)))
        o_s = cross_attn(qc_s, cache_mem_k[l].reshape(bs, n_mem, ca_w), cache_mem_v[l].reshape(bs, n_mem, ca_w))
        o_ca = jnp.concatenate([o_p.reshape(n_p, ca_w), o_s[:, :dec_seq].reshape(n_s, ca_w)], axis=0)
        c_out = matmul(o_ca, [ca_w_o], tm=1024, tn=1024, lead=l, name="ca_o")

        j = l // 2
        g_next = norm_mix_pre[l + 1] if l + 1 < depth else None
        if l % 2 == 0:
            x, h_ffn = add_norm(x, c_out, norm_ca_post[l], norm_ffn_pre[l])
            d_ff = ffn_w1.shape[2]
            hid = matmul(h_ffn, [ffn_w1, ffn_w3], out_dtype=BF16, tm=1024, tn=512, lead=j, single_buffer_w=True,
                         name="ffn_up")
            f_out = matmul(hid, [ffn_w2], tm=1024, tn=512, tk=_div_tile(d_ff, 2048), lead=j, name="ffn_down")
        else:
            x, _ = add_norm(x, c_out, norm_ca_post[l])
            f_out = moe_ffn(x, norm_ffn_pre[l], moe_router[j], moe_router_b[j], moe_w1[j], moe_w3[j], moe_w2[j])
        x, h = add_norm(x, f_out, norm_ffn_post[l], g_next)

        k0, v0 = fx0 + fx_w, fx0 + 2 * fx_w
        acc['p_k'].append(zb[:n_p, k0:k0 + fx_w].reshape(bp, seq, fx_h, HEAD_DIM))
        acc['p_v'].append(zb[:n_p, v0:v0 + fx_w].reshape(bp, seq, fx_h, HEAD_DIM))
        acc['p_lf'].append(lf_p.transpose(0, 2, 1))
        acc['p_mk'].append(mk.reshape(bp, n_mem, ca_h, HEAD_DIM))
        acc['p_mv'].append(mv.reshape(bp, n_mem, ca_h, HEAD_DIM))
        acc['p_conv'].append(conv_p)
        acc['p_C'].append(c_p)
        acc['p_n'].append(nrm_p)
        acc['p_m'].append(m_p)
        acc['p_S'].append(s_p)
        acc['s_k'].append(k_s.reshape(bs, dec_seq, fx_h, HEAD_DIM))
        acc['s_v'].append(v_s.reshape(bs, dec_seq, fx_h, HEAD_DIM))
        acc['s_lf'].append(lf_s[:, :, :dec_seq].transpose(0, 2, 1))
        acc['s_conv'].append(conv_s)
        acc['s_C'].append(c_s)
        acc['s_n'].append(nrm_s)
        acc['s_m'].append(m_s)
        acc['s_S'].append(s_s)

    y_p = x[:n_p].reshape(bp, seq, d_model)
    y_s = x[n_p:].reshape(bs, dec_seq, d_model)
    order = ('p_k', 'p_v', 'p_lf', 'p_mk', 'p_mv', 'p_conv', 'p_C', 'p_n', 'p_m', 'p_S',
             's_k', 's_v', 's_lf', 's_conv', 's_C', 's_n', 's_m', 's_S')
    return (y_p, y_s) + tuple(jnp.stack(acc[name]) for name in order)
```
